```python
import math
import jax
import jax.numpy as jnp
from jax import lax
import numpy as np

D_MODEL = 1024
BATCH = 16
SEQ = 2048
DEPTH = 2
DEC_BATCH = 32
DEC_SEQ = 1
PAST_LEN = 16384
PAGE_SIZE = 128

HEAD_DIM = 64
MOBA_HEADS = 4
MOBA_BLOCK = 256
MOBA_TOPK = 3
MOBA_QCH = 16
DIFF_HEADS = 4
DIFF_VDIM = 2 * HEAD_DIM
DIFF_QCH = 128
NSA_HEADS = 4
NSA_CMP_BLOCK = 32
NSA_CMP_STRIDE = 16
NSA_CMP_HIDDEN = 128
NSA_SEL_BLOCK = 64
NSA_SEL_TOPK = 16
NSA_WINDOW = 512
NSA_QCH = 32
WIN_QCH = 128
N_BUCKETS = 32
MAX_DISTANCE = 128
D_FF = 2688
N_BIAS_HEADS = MOBA_HEADS + DIFF_HEADS + NSA_HEADS
IN_SPLITS = (MOBA_HEADS * HEAD_DIM, MOBA_HEADS * HEAD_DIM, MOBA_HEADS * HEAD_DIM,
             DIFF_HEADS * 2 * HEAD_DIM, DIFF_HEADS * 2 * HEAD_DIM, DIFF_HEADS * DIFF_VDIM,
             NSA_HEADS * HEAD_DIM, 6 * HEAD_DIM, 3 * NSA_HEADS, 3 * D_MODEL)
N_IN = sum(IN_SPLITS)
EPS = 1e-6
NEG = -1e30
FORCE_BONUS = 1e4
SCALE = HEAD_DIM ** -0.5

kernel_name = 'hybrid_moba_diff_nsa_macaron_step'


def rms_norm(x, g):
    xf = x.astype(jnp.float32)
    y = xf * lax.rsqrt(jnp.mean(xf * xf, axis=-1, keepdims=True) + EPS)
    return (y * g.astype(jnp.float32)).astype(x.dtype)


def swiglu(u, w1, w2):
    a, b = jnp.split(u @ w1, 2, axis=-1)
    return (jax.nn.silu(a) * b) @ w2


def rel_bucket(dist):
    n = jnp.maximum(dist, 0)
    exact = N_BUCKETS // 2
    nf = jnp.maximum(n, 1).astype(jnp.float32)
    large = exact + (jnp.log(nf / exact) / math.log(MAX_DISTANCE / exact) * (N_BUCKETS - exact)).astype(jnp.int32)
    return jnp.where(n < exact, n, jnp.minimum(large, N_BUCKETS - 1))


def head_bias(table, dist):
    hidx = jnp.arange(table.shape[1]).reshape(-1, 1, 1)
    return table.T[hidx, rel_bucket(dist)].astype(jnp.float32)


def masked_softmax(logits, mask):
    logits = jnp.where(mask, logits, NEG)
    m = jnp.max(logits, axis=-1, keepdims=True)
    p = jnp.where(mask, jnp.exp(logits - m), 0.0)
    return p / jnp.maximum(jnp.sum(p, axis=-1, keepdims=True), 1e-30)


def split_in_proj(z):
    pts, acc = [], 0
    for s in IN_SPLITS[:-1]:
        acc += s
        pts.append(acc)
    return jnp.split(z, pts, axis=-1)


def mixer_inputs(u, w_in, qk_gain):
    B, S, _ = u.shape
    hd = HEAD_DIM
    mq, mk, mv, dq, dk, dv, nq, nkv, ngate, bgate = split_in_proj(u @ w_in)
    mq = rms_norm(mq.reshape(B, S, MOBA_HEADS, hd), qk_gain[0])
    mk = rms_norm(mk.reshape(B, S, MOBA_HEADS, hd), qk_gain[1])
    mv = mv.reshape(B, S, MOBA_HEADS, hd)
    dq = rms_norm(dq.reshape(B, S, DIFF_HEADS, 2, hd), qk_gain[2])
    dk = rms_norm(dk.reshape(B, S, DIFF_HEADS, 2, hd), qk_gain[3])
    dv = dv.reshape(B, S, DIFF_HEADS, DIFF_VDIM)
    nq = rms_norm(nq.reshape(B, S, NSA_HEADS, hd), qk_gain[4])
    nkv = nkv.reshape(B, S, 6, hd)
    sk = rms_norm(nkv[:, :, 2], qk_gain[6])
    wk = rms_norm(nkv[:, :, 4], qk_gain[7])
    m_row = jnp.stack([mk, mv], axis=2)
    d_row = jnp.stack([dk.reshape(B, S, DIFF_HEADS, 2 * hd), dv], axis=2)
    n_row = jnp.stack([nkv[:, :, 0], nkv[:, :, 1], sk, nkv[:, :, 3]], axis=2)
    w_row = jnp.stack([wk, nkv[:, :, 5]], axis=2)
    return mq, dq, nq, ngate, bgate, m_row, d_row, n_row, w_row


def gather_rows(pool_l, page_table, new_rows, pos, head=None):
    b = jnp.arange(pos.shape[0]).reshape((-1,) + (1,) * (pos.ndim - 1))
    pc = jnp.clip(pos, 0, PAST_LEN - 1)
    phys = page_table[b, pc // PAGE_SIZE]
    pn = jnp.clip(pos - PAST_LEN, 0, new_rows.shape[1] - 1)
    if head is None:
        past = pool_l[phys, pc % PAGE_SIZE]
        new = new_rows[b, pn]
    else:
        past = pool_l[phys, pc % PAGE_SIZE, :, head]
        new = new_rows[b, pn, :, head]
    is_past = (pos < PAST_LEN).reshape(pos.shape + (1,) * (past.ndim - pos.ndim))
    return jnp.where(is_past, past, new)


def block_means(k, nb):
    B, L, H, d = k.shape
    kb = jnp.pad(k, ((0, 0), (0, nb * MOBA_BLOCK - L), (0, 0), (0, 0))).reshape(B, nb, MOBA_BLOCK, H, d)
    return jnp.mean(kb.astype(jnp.float32), axis=2), kb


def moba_select(q, kmean, own):
    nb = kmean.shape[1]
    gate = jnp.einsum('bqhd,bnhd->bhqn', q, kmean, preferred_element_type=jnp.float32)
    gate = jnp.where(jnp.arange(nb)[None, :] < own[:, None], gate, NEG)
    _, sel = lax.top_k(gate, min(MOBA_TOPK, nb))
    return sel


def moba_prompt(q, m_row, table):
    B, S, H, d = q.shape
    nb = -(-S // MOBA_BLOCK)
    kmean, kb = block_means(m_row[:, :, 0], nb)
    _, vb = block_means(m_row[:, :, 1], nb)
    t = jnp.arange(S)
    sel = moba_select(q, kmean, t // MOBA_BLOCK)
    n = sel.shape[-1]
    bi = jnp.arange(B).reshape(B, 1, 1, 1)
    hi = jnp.arange(H).reshape(1, H, 1, 1)
    ar = jnp.arange(MOBA_BLOCK)

    def chunk(c):
        t0 = c * MOBA_QCH
        tc = t0 + jnp.arange(MOBA_QCH)
        own = t0 // MOBA_BLOCK
        qc = lax.dynamic_slice_in_dim(q, t0, MOBA_QCH, axis=1)
        sc = lax.dynamic_slice_in_dim(sel, t0, MOBA_QCH, axis=2)
        ks = kb[bi, sc, :, hi].reshape(B, H, MOBA_QCH, n * MOBA_BLOCK, d)
        vs = vb[bi, sc, :, hi].reshape(B, H, MOBA_QCH, n * MOBA_BLOCK, d)
        pos_sel = (sc[..., None] * MOBA_BLOCK + ar).reshape(B, H, MOBA_QCH, n * MOBA_BLOCK)
        ok_sel = jnp.broadcast_to((sc < own)[..., None], sc.shape + (MOBA_BLOCK,)).reshape(pos_sel.shape)
        ko = lax.dynamic_index_in_dim(kb, own, axis=1, keepdims=False)
        vo = lax.dynamic_index_in_dim(vb, own, axis=1, keepdims=False)
        pos_own = own * MOBA_BLOCK + ar
        l_sel = jnp.einsum('bqhd,bhqkd->bhqk', qc, ks, preferred_element_type=jnp.float32) * SCALE
        l_sel = l_sel + head_bias(table, tc[:, None] - pos_sel)
        l_own = jnp.einsum('bqhd,bkhd->bhqk', qc, ko, preferred_element_type=jnp.float32) * SCALE
        l_own = l_own + head_bias(table, tc[:, None] - pos_own[None, :])
        ok_own = jnp.broadcast_to(pos_own[None, :] <= tc[:, None], (B, H, MOBA_QCH, MOBA_BLOCK))
        p = masked_softmax(jnp.concatenate([l_sel, l_own], -1), jnp.concatenate([ok_sel, ok_own], -1))
        nk = n * MOBA_BLOCK
        return (jnp.einsum('bhqk,bhqkd->bqhd', p[..., :nk], vs)
                + jnp.einsum('bhqk,bkhd->bqhd', p[..., nk:], vo))

    o = lax.map(chunk, jnp.arange(S // MOBA_QCH))
    return jnp.moveaxis(o, 0, 1).reshape(B, S, H * d)


def moba_decode(q, m_new, pool_l, page_table, table):
    DB, Q, H, d = q.shape
    L = PAST_LEN + Q
    t = PAST_LEN + jnp.arange(Q)
    k_past = pool_l[page_table, :, 0].reshape(DB, PAST_LEN, H, d)
    kmean, _ = block_means(jnp.concatenate([k_past, m_new[:, :, 0]], axis=1), -(-L // MOBA_BLOCK))
    own = t // MOBA_BLOCK
    sel = moba_select(q, kmean, own)
    n = sel.shape[-1]
    ar = jnp.arange(MOBA_BLOCK)
    pos_sel = (sel[..., None] * MOBA_BLOCK + ar).reshape(DB, H, Q, n * MOBA_BLOCK)
    ok_sel = jnp.broadcast_to((sel < own[:, None])[..., None], sel.shape + (MOBA_BLOCK,)).reshape(pos_sel.shape)
    pos_own = own[:, None] * MOBA_BLOCK + ar
    pos = jnp.concatenate([pos_sel, jnp.broadcast_to(pos_own, (DB, H, Q, MOBA_BLOCK))], -1)
    mask = jnp.concatenate([ok_sel, jnp.broadcast_to(pos_own <= t[:, None], (DB, H, Q, MOBA_BLOCK))], -1)
    kv = gather_rows(pool_l, page_table, m_new, pos, jnp.arange(H).reshape(1, H, 1, 1))
    logits = jnp.einsum('bqhd,bhqkd->bhqk', q, kv[..., 0, :], preferred_element_type=jnp.float32) * SCALE
    p = masked_softmax(logits + head_bias(table, t[:, None] - pos), mask)
    o = jnp.einsum('bhqk,bhqkd->bqhd', p, kv[..., 1, :])
    return o.reshape(DB, Q, H * d)


def diff_combine(om, lam, lam_init, subln):
    lamf = lam.astype(jnp.float32)
    lam_full = jnp.exp(jnp.sum(lamf[0] * lamf[1])) - jnp.exp(jnp.sum(lamf[2] * lamf[3])) + lam_init
    o = om[..., 0, :] - lam_full * om[..., 1, :]
    o = rms_norm(o, subln) * (1.0 - lam_init)
    return o.reshape(o.shape[:-2] + (-1,))


def diff_prompt(q, d_row, table, lam, lam_init, subln):
    B, S, H, _, d = q.shape
    k = d_row[:, :, 0].reshape(B, S, H, 2, d)
    v = d_row[:, :, 1]
    kpos = jnp.arange(S)

    def chunk(c):
        t0 = c * DIFF_QCH
        tc = t0 + jnp.arange(DIFF_QCH)
        qc = lax.dynamic_slice_in_dim(q, t0, DIFF_QCH, axis=1)
        logits = jnp.einsum('bqhmd,bkhmd->bhmqk', qc, k, preferred_element_type=jnp.float32) * SCALE
        logits = logits + head_bias(table, tc[:, None] - kpos[None, :])[:, None]
        p = masked_softmax(logits, kpos[None, :] <= tc[:, None])
        return jnp.einsum('bhmqk,bkhe->bqhme', p, v)

    om = lax.map(chunk, jnp.arange(S // DIFF_QCH))
    om = jnp.moveaxis(om, 0, 1).reshape(B, S, H, 2, DIFF_VDIM)
    return diff_combine(om, lam, lam_init, subln)


def diff_decode(q, d_row, pool_l, page_table, table, lam, lam_init, subln):
    DB, Q, H, _, d = q.shape
    t = PAST_LEN + jnp.arange(Q)

    def update(carry, k, v, pos):
        m, s, acc = carry
        logits = jnp.einsum('bqhmd,bkhmd->bhmqk', q, k, preferred_element_type=jnp.float32) * SCALE
        logits = logits + head_bias(table, t[:, None] - pos[None, :])[:, None]
        mask = pos[None, :] <= t[:, None]
        logits = jnp.where(mask, logits, NEG)
        m_new = jnp.maximum(m, jnp.max(logits, axis=-1))
        p = jnp.where(mask, jnp.exp(logits - m_new[..., None]), 0.0)
        corr = jnp.exp(m - m_new)
        s = s * corr + jnp.sum(p, axis=-1)
        acc = acc * corr[..., None] + jnp.einsum('bhmqk,bkhe->bhmqe', p, v, preferred_element_type=jnp.float32)
        return (m_new, s, acc)

    def page_step(carry, xs_):
        j, phys = xs_
        kv = pool_l[phys]
        k = kv[:, :, 0].reshape(DB, PAGE_SIZE, H, 2, d)
        return update(carry, k, kv[:, :, 1], j * PAGE_SIZE + jnp.arange(PAGE_SIZE)), None

    init = (jnp.full((DB, H, 2, Q), NEG, jnp.float32), jnp.zeros((DB, H, 2, Q), jnp.float32),
            jnp.zeros((DB, H, 2, Q, DIFF_VDIM), jnp.float32))
    carry, _ = lax.scan(page_step, init, (jnp.arange(page_table.shape[1]), page_table.T))
    _, s, acc = update(carry, d_row[:, :, 0].reshape(DB, Q, H, 2, d), d_row[:, :, 1], t)
    om = jnp.moveaxis(acc / s[..., None], 3, 1)
    return diff_combine(om, lam, lam_init, subln)


def compress(rows, pos_emb, w1, w2):
    B, L, d = rows.shape
    n_cmp = (L - NSA_CMP_BLOCK) // NSA_CMP_STRIDE + 1
    idx = jnp.arange(n_cmp)[:, None] * NSA_CMP_STRIDE + jnp.arange(NSA_CMP_BLOCK)[None, :]
    blk = (rows[:, idx] + pos_emb).reshape(B, n_cmp, NSA_CMP_BLOCK * d)
    return jax.nn.gelu(blk @ w1) @ w2


def nsa_compressed(q, ck, cv, t, g_kc, cmp_pos, cmp_w1, cmp_w2):
    kc = rms_norm(compress(ck, cmp_pos[0], cmp_w1[0], cmp_w2[0]), g_kc)
    vc = compress(cv, cmp_pos[1], cmp_w1[1], cmp_w2[1])
    starts = jnp.arange(kc.shape[1]) * NSA_CMP_STRIDE
    logits = jnp.einsum('bqhd,bnd->bhqn', q, kc, preferred_element_type=jnp.float32) * SCALE
    p = masked_softmax(logits, (starts + NSA_CMP_BLOCK - 1)[None, :] <= t[:, None])
    o = jnp.einsum('bhqn,bnd->bqhd', p, vc)
    lo = jnp.arange(-(-ck.shape[1] // NSA_SEL_BLOCK)) * NSA_SEL_BLOCK
    overlap = ((starts[:, None] < lo[None, :] + NSA_SEL_BLOCK)
               & (starts[:, None] + NSA_CMP_BLOCK > lo[None, :])).astype(jnp.float32)
    imp = jnp.einsum('bhqn,nj->bqj', p, overlap)
    return o, imp


def nsa_select_blocks(imp, t):
    j = jnp.arange(imp.shape[-1])[None, :]
    cur = (t // NSA_SEL_BLOCK)[:, None]
    forced = (j == 0) | (j == cur) | (j == cur - 1)
    score = jnp.where(j <= cur, imp + FORCE_BONUS * forced, NEG)
    _, idx = lax.top_k(score, min(NSA_SEL_TOPK, imp.shape[-1]))
    return idx


def nsa_sel_attend(q, ks, vs, pos, t, table):
    dist = (t[None, :, None] - pos)[:, None]
    logits = jnp.einsum('bqhd,bqkd->bhqk', q, ks, preferred_element_type=jnp.float32) * SCALE
    p = masked_softmax(logits + head_bias(table, dist), dist >= 0)
    return jnp.einsum('bhqk,bqkd->bqhd', p, vs)


def nsa_window_prompt(q, w_row, table):
    B, S, H, d = q.shape
    nb = S // WIN_QCH
    nw = NSA_WINDOW // WIN_QCH
    kvp = jnp.pad(w_row, ((0, 0), (NSA_WINDOW, 0), (0, 0), (0, 0))).reshape(B, nb + nw, WIN_QCH, 2, d)
    kvw = jnp.concatenate([kvp[:, i:i + nb] for i in range(nw + 1)], axis=2)
    qb = q.reshape(B, nb, WIN_QCH, H, d)
    tq = jnp.arange(nb)[:, None] * WIN_QCH + jnp.arange(WIN_QCH)[None, :]
    pk = jnp.arange(nb)[:, None] * WIN_QCH - NSA_WINDOW + jnp.arange((nw + 1) * WIN_QCH)[None, :]
    dist = tq[:, :, None] - pk[:, None, :]
    mask = (dist >= 0) & (dist < NSA_WINDOW) & (pk[:, None, :] >= 0)
    logits = jnp.einsum('bnqhd,bnkd->bnhqk', qb, kvw[..., 0, :], preferred_element_type=jnp.float32) * SCALE
    p = masked_softmax(logits + head_bias(table, dist[:, None]), mask[:, None])
    return jnp.einsum('bnhqk,bnkd->bnqhd', p, kvw[..., 1, :]).reshape(B, S, H, d)


def nsa_window_decode(q, buf, w_new, t, table):
    w_buf = buf.shape[1]
    kv = jnp.concatenate([buf, w_new], axis=1)
    pos = PAST_LEN - w_buf + jnp.arange(kv.shape[1])
    dist = t[:, None] - pos[None, :]
    logits = jnp.einsum('bqhd,bkd->bhqk', q, kv[:, :, 0], preferred_element_type=jnp.float32) * SCALE
    p = masked_softmax(logits + head_bias(table, dist), (dist >= 0) & (dist < NSA_WINDOW))
    o = jnp.einsum('bhqk,bkd->bqhd', p, kv[:, :, 1])
    return o, kv[:, kv.shape[1] - w_buf:]


def nsa_combine(ngate, o_cmp, o_sel, o_win):
    B, S, H, d = o_cmp.shape
    g = jax.nn.sigmoid(ngate.astype(jnp.float32)).reshape(B, S, 3, H, 1)
    o = g[:, :, 0] * o_cmp + g[:, :, 1] * o_sel + g[:, :, 2] * o_win
    return o.reshape(B, S, H * d)


def nsa_prompt(q, n_row, w_row, ngate, table, g_kc, cmp_pos, cmp_w1, cmp_w2):
    B, S, H, d = q.shape
    t = jnp.arange(S)
    o_cmp, imp = nsa_compressed(q, n_row[:, :, 0], n_row[:, :, 1], t, g_kc, cmp_pos, cmp_w1, cmp_w2)
    idx = nsa_select_blocks(imp, t)
    skb = n_row[:, :, 2].reshape(B, S // NSA_SEL_BLOCK, NSA_SEL_BLOCK, d)
    svb = n_row[:, :, 3].reshape(B, S // NSA_SEL_BLOCK, NSA_SEL_BLOCK, d)
    bi = jnp.arange(B).reshape(B, 1, 1)

    def chunk(c):
        t0 = c * NSA_QCH
        qc = lax.dynamic_slice_in_dim(q, t0, NSA_QCH, axis=1)
        ic = lax.dynamic_slice_in_dim(idx, t0, NSA_QCH, axis=1)
        ks = skb[bi, ic].reshape(B, NSA_QCH, -1, d)
        vs = svb[bi, ic].reshape(B, NSA_QCH, -1, d)
        pos = (ic[..., None] * NSA_SEL_BLOCK + jnp.arange(NSA_SEL_BLOCK)).reshape(B, NSA_QCH, -1)
        return nsa_sel_attend(qc, ks, vs, pos, t0 + jnp.arange(NSA_QCH), table)

    o_sel = jnp.moveaxis(lax.map(chunk, jnp.arange(S // NSA_QCH)), 0, 1).reshape(B, S, H, d)
    o_win = nsa_window_prompt(q, w_row, table)
    return nsa_combine(ngate, o_cmp, o_sel, o_win)


def nsa_decode(q, n_new, w_new, ngate, pool_l, page_table, buf, table, g_kc, cmp_pos, cmp_w1, cmp_w2):
    DB, Q, H, d = q.shape
    t = PAST_LEN + jnp.arange(Q)
    past_c = pool_l[page_table, :, :2].reshape(DB, PAST_LEN, 2, d)
    all_c = jnp.concatenate([past_c, n_new[:, :, :2]], axis=1)
    o_cmp, imp = nsa_compressed(q, all_c[:, :, 0], all_c[:, :, 1], t, g_kc, cmp_pos, cmp_w1, cmp_w2)
    idx = nsa_select_blocks(imp, t)
    pos = (idx[..., None] * NSA_SEL_BLOCK + jnp.arange(NSA_SEL_BLOCK)).reshape(DB, Q, -1)
    rows = gather_rows(pool_l, page_table, n_new, pos)
    o_sel = nsa_sel_attend(q, rows[..., 2, :], rows[..., 3, :], pos, t, table)
    o_win, new_buf = nsa_window_decode(q, buf, w_new, t, table)
    return nsa_combine(ngate, o_cmp, o_sel, o_win), new_buf


def merge(o_a, o_b, o_c, bgate, wa, wb, wc, w_out):
    dt = bgate.dtype
    ga, gb, gc = jnp.split(jax.nn.sigmoid(bgate), 3, axis=-1)
    h = ga * (o_a.astype(dt) @ wa) + gb * (o_b.astype(dt) @ wb) + gc * (o_c.astype(dt) @ wc)
    return h @ w_out


def setup_inputs(seed: int = 0) -> dict:
    key = jax.random.key(seed)
    ks = jax.random.split(key, 24)

    def nrm(k, shape, scale):
        return jax.random.normal(k, shape, jnp.float32) * scale

    n_pages = PAST_LEN // PAGE_SIZE
    n_used = DEC_BATCH * n_pages
    n_pool = n_used + n_used // 4
    page_table = jax.random.permutation(ks[0], n_pool)[:n_used].reshape(DEC_BATCH, n_pages).astype(jnp.int32)
    w_buf = min(NSA_WINDOW, PAST_LEN)
    return {
        'x_prompt': nrm(ks[1], (BATCH, SEQ, D_MODEL), 1.0),
        'x_sample': nrm(ks[2], (DEC_BATCH, DEC_SEQ, D_MODEL), 1.0),
        'cache_moba_kv': nrm(ks[3], (DEPTH, n_pool, PAGE_SIZE, 2, MOBA_HEADS, HEAD_DIM), 1.0),
        'cache_diff_kv': nrm(ks[4], (DEPTH, n_pool, PAGE_SIZE, 2, DIFF_HEADS, DIFF_VDIM), 1.0),
        'cache_nsa_kv': nrm(ks[5], (DEPTH, n_pool, PAGE_SIZE, 4, HEAD_DIM), 1.0),
        'state_nsa_win': nrm(ks[6], (DEPTH, DEC_BATCH, w_buf, 2, HEAD_DIM), 1.0),
        'page_table': page_table,
        'rel_bias': nrm(ks[7], (N_BUCKETS, N_BIAS_HEADS), 0.5),
        'norm_gain': 1.0 + nrm(ks[8], (DEPTH, 3, D_MODEL), 0.05),
        'ffn_w1': nrm(ks[9], (DEPTH, 2, D_MODEL, 2 * D_FF), D_MODEL ** -0.5),
        'ffn_w2': nrm(ks[10], (DEPTH, 2, D_FF, D_MODEL), D_FF ** -0.5),
        'w_in': nrm(ks[11], (DEPTH, D_MODEL, N_IN), D_MODEL ** -0.5),
        'qk_gain': 1.0 + nrm(ks[12], (DEPTH, 8, HEAD_DIM), 0.05),
        'diff_lambda': nrm(ks[13], (DEPTH, 4, HEAD_DIM), 0.1),
        'diff_subln': 1.0 + nrm(ks[14], (DEPTH, DIFF_VDIM), 0.05),
        'cmp_pos': nrm(ks[15], (DEPTH, 2, NSA_CMP_BLOCK, HEAD_DIM), 0.1),
        'cmp_w1': nrm(ks[16], (DEPTH, 2, NSA_CMP_BLOCK * HEAD_DIM, NSA_CMP_HIDDEN), (NSA_CMP_BLOCK * HEAD_DIM) ** -0.5),
        'cmp_w2': nrm(ks[17], (DEPTH, 2, NSA_CMP_HIDDEN, HEAD_DIM), NSA_CMP_HIDDEN ** -0.5),
        'w_branch_moba': nrm(ks[18], (DEPTH, MOBA_HEADS * HEAD_DIM, D_MODEL), (MOBA_HEADS * HEAD_DIM) ** -0.5),
        'w_branch_diff': nrm(ks[19], (DEPTH, DIFF_HEADS * DIFF_VDIM, D_MODEL), (DIFF_HEADS * DIFF_VDIM) ** -0.5),
        'w_branch_nsa': nrm(ks[20], (DEPTH, NSA_HEADS * HEAD_DIM, D_MODEL), (NSA_HEADS * HEAD_DIM) ** -0.5),
        'w_out': nrm(ks[21], (DEPTH, D_MODEL, D_MODEL), D_MODEL ** -0.5),
    }


def reference(x_prompt, x_sample, cache_moba_kv, cache_diff_kv, cache_nsa_kv, state_nsa_win, page_table,
              rel_bias, norm_gain, ffn_w1, ffn_w2, w_in, qk_gain, diff_lambda, diff_subln,
              cmp_pos, cmp_w1, cmp_w2, w_branch_moba, w_branch_diff, w_branch_nsa, w_out):
    tab_m = rel_bias[:, :MOBA_HEADS]
    tab_d = rel_bias[:, MOBA_HEADS:MOBA_HEADS + DIFF_HEADS]
    tab_n = rel_bias[:, MOBA_HEADS + DIFF_HEADS:]
    xp, xs = x_prompt, x_sample
    mkv_p, mkv_s, dkv_p, dkv_s, nkv_p, nkv_s, win_p, win_s = [], [], [], [], [], [], [], []
    for l in range(DEPTH):
        lam_init = 0.8 - 0.6 * math.exp(-0.3 * l)
        xp = xp + 0.5 * swiglu(rms_norm(xp, norm_gain[l, 0]), ffn_w1[l, 0], ffn_w2[l, 0])
        xs = xs + 0.5 * swiglu(rms_norm(xs, norm_gain[l, 0]), ffn_w1[l, 0], ffn_w2[l, 0])
        mq, dq, nq, ngate, bgate, m_row, d_row, n_row, w_row = mixer_inputs(
            rms_norm(xp, norm_gain[l, 1]), w_in[l], qk_gain[l])
        o_a = moba_prompt(mq, m_row, tab_m)
        o_b = diff_prompt(dq, d_row, tab_d, diff_lambda[l], lam_init, diff_subln[l])
        o_c = nsa_prompt(nq, n_row, w_row, ngate, tab_n, qk_gain[l, 5], cmp_pos[l], cmp_w1[l], cmp_w2[l])
        xp = xp + merge(o_a, o_b, o_c, bgate, w_branch_moba[l], w_branch_diff[l], w_branch_nsa[l], w_out[l])
        mkv_p.append(m_row)
        dkv_p.append(d_row)
        nkv_p.append(n_row)
        win_p.append(w_row[:, w_row.shape[1] - min(NSA_WINDOW, w_row.shape[1]):])
        mq, dq, nq, ngate, bgate, m_row, d_row, n_row, w_row = mixer_inputs(
            rms_norm(xs, norm_gain[l, 1]), w_in[l], qk_gain[l])
        o_a = moba_decode(mq, m_row, cache_moba_kv[l], page_table, tab_m)
        o_b = diff_decode(dq, d_row, cache_diff_kv[l], page_table, tab_d, diff_lambda[l], lam_init, diff_subln[l])
        o_c, buf = nsa_decode(nq, n_row, w_row, ngate, cache_nsa_kv[l], page_table, state_nsa_win[l], tab_n,
                              qk_gain[l, 5], cmp_pos[l], cmp_w1[l], cmp_w2[l])
        xs = xs + merge(o_a, o_b, o_c, bgate, w_branch_moba[l], w_branch_diff[l], w_branch_nsa[l], w_out[l])
        mkv_s.append(m_row)
        dkv_s.append(d_row)
        nkv_s.append(n_row)
        win_s.append(buf)
        xp = xp + 0.5 * swiglu(rms_norm(xp, norm_gain[l, 2]), ffn_w1[l, 1], ffn_w2[l, 1])
        xs = xs + 0.5 * swiglu(rms_norm(xs, norm_gain[l, 2]), ffn_w1[l, 1], ffn_w2[l, 1])
    return (xp, xs, jnp.stack(mkv_p), jnp.stack(mkv_s), jnp.stack(dkv_p), jnp.stack(dkv_s),
            jnp.stack(nkv_p), jnp.stack(nkv_s), jnp.stack(win_p), jnp.stack(win_s))
```

```python
import functools
import math

import numpy as np
import jax
import jax.numpy as jnp
from jax import lax
from jax.experimental import pallas as pl
from jax.experimental.pallas import tpu as pltpu

F32 = jnp.float32
BF16 = jnp.bfloat16
HI = lax.Precision.HIGHEST

D_MODEL = 1024
DEPTH = 2
PAST_LEN = 16384
PAGE_SIZE = 128
HEAD_DIM = 64
MOBA_HEADS = 4
MOBA_BLOCK = 256
MOBA_TOPK = 3
DIFF_HEADS = 4
DIFF_VDIM = 2 * HEAD_DIM
NSA_HEADS = 4
NSA_CMP_BLOCK = 32
NSA_CMP_STRIDE = 16
NSA_CMP_HIDDEN = 128
NSA_SEL_BLOCK = 64
NSA_SEL_TOPK = 16
NSA_WINDOW = 512
N_BUCKETS = 32
MAX_DISTANCE = 128
D_FF = 2688
EPS = 1e-6
NEG = -1e30
FORCE_BONUS = 1e4
SCALE = HEAD_DIM ** -0.5

TQ = 256
N_MAIN = 2944
N_PACK = 3072
VMEM_LIMIT = 56 * 1024 * 1024


def _cparams(sem):
    return pltpu.CompilerParams(dimension_semantics=sem, vmem_limit_bytes=VMEM_LIMIT)


def _rms(x, g):
    return x * lax.rsqrt(jnp.mean(x * x, axis=-1, keepdims=True) + EPS) * g


def _dot_nt(a, b, precision=None):
    return lax.dot_general(a, b, (((1,), (1,)), ((), ())), preferred_element_type=F32, precision=precision)


def _ffn_kernel(x_ref, g_ref, w1a_ref, w1b_ref, w2_ref, o_ref, u_scr, acc_scr):
    f = pl.program_id(1)

    @pl.when(f == 0)
    def _():
        u_scr[...] = _rms(x_ref[...], g_ref[...]).astype(BF16)
        acc_scr[...] = jnp.zeros_like(acc_scr)

    u = u_scr[...]
    a = jnp.dot(u, w1a_ref[...], preferred_element_type=F32)
    b = jnp.dot(u, w1b_ref[...], preferred_element_type=F32)
    h = (a * jax.nn.sigmoid(a) * b).astype(BF16)
    acc_scr[...] += jnp.dot(h, w2_ref[...], preferred_element_type=F32)

    @pl.when(f == pl.num_programs(1) - 1)
    def _():
        o_ref[...] = x_ref[...] + 0.5 * acc_scr[...]


def _ffn(x, g, w1, w2, tm):
    n = x.shape[0]
    tf = 896
    nf = D_FF // tf
    return pl.pallas_call(
        _ffn_kernel,
        grid=(n // tm, nf),
        in_specs=[
            pl.BlockSpec((tm, D_MODEL), lambda i, f: (i, 0)),
            pl.BlockSpec((1, D_MODEL), lambda i, f: (0, 0)),
            pl.BlockSpec((D_MODEL, tf), lambda i, f: (0, f)),
            pl.BlockSpec((D_MODEL, tf), lambda i, f: (0, f + nf)),
            pl.BlockSpec((tf, D_MODEL), lambda i, f: (f, 0)),
        ],
        out_specs=pl.BlockSpec((tm, D_MODEL), lambda i, f: (i, 0)),
        out_shape=jax.ShapeDtypeStruct((n, D_MODEL), F32),
        scratch_shapes=[pltpu.VMEM((tm, D_MODEL), BF16), pltpu.VMEM((tm, D_MODEL), F32)],
        compiler_params=_cparams(("parallel", "arbitrary")),
        name="ffn",
    )(x, g, w1, w1, w2)


_SLABS = (
    (0, 256, 0, 0),
    (256, 256, 1, 0),
    (512, 256, 1, 256),
    (768, 256, 2, 0),
    (1024, 256, 2, 256),
    (1280, 256, 3, 0),
    (1536, 256, 3, 256),
    (1792, 256, 3, 512),
    (2048, 256, 3, 768),
    (2304, 256, 4, 0),
    (2560, 256, 5, 0),
    (2816, 128, 6, 0),
    (2944, 128, 7, 0),
)


def _inproj_kernel(x_ref, g_ref, w_ref, gain_ref, nmask_ref, bd_ref,
                   mq_ref, mrow_ref, dq_ref, drow_ref, nq_ref, nrow_ref, wrow_ref, ngate_ref):
    outs = (mq_ref, mrow_ref, dq_ref, drow_ref, nq_ref, nrow_ref, wrow_ref, ngate_ref)
    u = _rms(x_ref[...], g_ref[...]).astype(BF16)
    for c0, w, oi, oc in _SLABS:
        z = jnp.dot(u, w_ref[:, c0:c0 + w], preferred_element_type=F32)
        if c0 < N_MAIN and c0 not in (512, 1792, 2048):
            ss = jnp.dot((z * z).astype(BF16), bd_ref[0:w, 0:w], preferred_element_type=F32)
            zn = z * lax.rsqrt(ss * (1.0 / HEAD_DIM) + EPS) * gain_ref[:, c0:c0 + w]
            z = jnp.where(nmask_ref[:, c0:c0 + w] > 0.5, zn, z)
        outs[oi][:, oc:oc + w] = z


def _inproj(x, g, w_pack, gain_row, nmask_row, bd, tm):
    n = x.shape[0]
    widths = (256, 512, 512, 1024, 256, 256, 128, 128)
    const = lambda i: (0, 0)
    return pl.pallas_call(
        _inproj_kernel,
        grid=(n // tm,),
        in_specs=[
            pl.BlockSpec((tm, D_MODEL), lambda i: (i, 0)),
            pl.BlockSpec((1, D_MODEL), const),
            pl.BlockSpec((D_MODEL, N_PACK), const),
            pl.BlockSpec((1, N_MAIN), const),
            pl.BlockSpec((1, N_MAIN), const),
            pl.BlockSpec((256, 256), const),
        ],
        out_specs=[pl.BlockSpec((tm, w), lambda i: (i, 0)) for w in widths],
        out_shape=[jax.ShapeDtypeStruct((n, w), F32) for w in widths],
        compiler_params=_cparams(("parallel",)),
        name="inproj",
    )(x, g, w_pack, gain_row, nmask_row, bd)


def _tri_tables(nq, depth=None):
    ti, tj = [], []
    for i in range(nq):
        lo = 0 if depth is None else max(0, i - depth + 1)
        for j in range(i, lo - 1, -1):
            ti.append(i)
            tj.append(j)
    return jnp.asarray(ti, jnp.int32), jnp.asarray(tj, jnp.int32)


def _softmax_step(s, v, m_ref, l_ref, acc_ref, c):
    m_old = m_ref[c]
    m_new = jnp.maximum(m_old, jnp.max(s, axis=-1, keepdims=True))
    alpha = jnp.exp(m_old - m_new)
    p = jnp.exp(s - m_new)
    l_ref[c] = alpha * l_ref[c] + jnp.sum(p, axis=-1, keepdims=True)
    acc_ref[c] = alpha * acc_ref[c] + jnp.dot(p.astype(BF16), v, preferred_element_type=F32)
    m_ref[c] = m_new


def _init_state(m_ref, l_ref, acc_ref):
    m_ref[...] = jnp.full(m_ref.shape, NEG, F32)
    l_ref[...] = jnp.zeros(l_ref.shape, F32)
    acc_ref[...] = jnp.zeros(acc_ref.shape, F32)


def _diff_kernel(ti_ref, tj_ref, q_ref, k_ref, v_ref, bias_ref, lam_ref, subln_ref, o_ref,
                 qs_ref, m_ref, l_ref, acc_ref, *, lam_init):
    t = pl.program_id(1)
    i = ti_ref[t]
    j = tj_ref[t]

    @pl.when(j == i)
    def _():
        q = q_ref[0] * SCALE
        chunk = lax.broadcasted_iota(jnp.int32, (1, 256), 1) // HEAD_DIM
        for c in range(8):
            half = q[:, 256 * (c // 4):256 * (c // 4) + 256]
            qs_ref[c] = jnp.where(chunk == c % 4, half, 0.0).astype(BF16)
        _init_state(m_ref, l_ref, acc_ref)

    kt = k_ref[0].astype(BF16)
    vt = v_ref[0].astype(BF16)
    tsel = jnp.minimum(i - j, 2)
    for h in range(DIFF_HEADS):
        bias = bias_ref[h, tsel]
        kh = kt[:, 256 * (h // 2):256 * (h // 2) + 256]
        vh = vt[:, DIFF_VDIM * h:DIFF_VDIM * (h + 1)]
        for m in range(2):
            c = 2 * h + m
            _softmax_step(_dot_nt(qs_ref[c], kh) + bias, vh, m_ref, l_ref, acc_ref, c)

    @pl.when(j == 0)
    def _():
        lam = lam_ref[...]
        lam_full = (jnp.exp(jnp.sum(lam[0:1] * lam[1:2], axis=-1, keepdims=True))
                    - jnp.exp(jnp.sum(lam[2:3] * lam[3:4], axis=-1, keepdims=True)) + lam_init)
        for h in range(DIFF_HEADS):
            o0 = acc_ref[2 * h] / l_ref[2 * h]
            o1 = acc_ref[2 * h + 1] / l_ref[2 * h + 1]
            o = _rms(o0 - lam_full * o1, subln_ref[...]) * (1.0 - lam_init)
            o_ref[0, :, DIFF_VDIM * h:DIFF_VDIM * (h + 1)] = o


def _diff_prompt(dq, d_row, bias_tiles, lam, subln, lam_init):
    b, s, _ = dq.shape
    ti, tj = _tri_tables(s // TQ)
    grid_spec = pltpu.PrefetchScalarGridSpec(
        num_scalar_prefetch=2,
        grid=(b, int(ti.shape[0])),
        in_specs=[
            pl.BlockSpec((1, TQ, 512), lambda b_, t, ti_, tj_: (b_, ti_[t], 0)),
            pl.BlockSpec((1, TQ, 512), lambda b_, t, ti_, tj_: (b_, tj_[t], 0)),
            pl.BlockSpec((1, TQ, 512), lambda b_, t, ti_, tj_: (b_, tj_[t], 1)),
            pl.BlockSpec((4, 3, TQ, TQ), lambda b_, t, ti_, tj_: (1, 0, 0, 0)),
            pl.BlockSpec((4, HEAD_DIM), lambda b_, t, ti_, tj_: (0, 0)),
            pl.BlockSpec((1, DIFF_VDIM), lambda b_, t, ti_, tj_: (0, 0)),
        ],
        out_specs=pl.BlockSpec((1, TQ, 512), lambda b_, t, ti_, tj_: (b_, ti_[t], 0)),
        scratch_shapes=[
            pltpu.VMEM((8, TQ, 256), BF16),
            pltpu.VMEM((8, TQ, 1), F32),
            pltpu.VMEM((8, TQ, 1), F32),
            pltpu.VMEM((8, TQ, DIFF_VDIM), F32),
        ],
    )
    return pl.pallas_call(
        functools.partial(_diff_kernel, lam_init=lam_init),
        grid_spec=grid_spec,
        out_shape=jax.ShapeDtypeStruct((b, s, 512), F32),
        compiler_params=_cparams(("parallel", "arbitrary")),
        name="diff_prompt",
    )(ti, tj, dq, d_row, d_row, bias_tiles, lam, subln)


def _rank_desc(score, n):
    col = lax.broadcasted_iota(jnp.int32, (1, n), 1)
    rank = jnp.zeros(score.shape, F32)
    for jp in range(n):
        sj = score[:, jp:jp + 1]
        tie = jnp.where(jp < col, 1.0, 0.0)
        rank = rank + jnp.where(sj > score, 1.0, jnp.where(sj == score, tie, 0.0))
    return rank


def _kmean_kernel(k_ref, o_ref):
    nb = o_ref.shape[1]
    o_ref[0] = jnp.mean(k_ref[0].reshape(nb, MOBA_BLOCK, 256), axis=1)


def _moba_kmean(m_row):
    b, s, _ = m_row.shape
    nb = s // MOBA_BLOCK
    return pl.pallas_call(
        _kmean_kernel,
        grid=(b,),
        in_specs=[pl.BlockSpec((1, s, 256), lambda i: (i, 0, 0))],
        out_specs=pl.BlockSpec((1, nb, 256), lambda i: (i, 0, 0)),
        out_shape=jax.ShapeDtypeStruct((b, nb, 256), F32),
        compiler_params=_cparams(("parallel",)),
        name="moba_kmean",
    )(m_row)


def _moba_kernel(ti_ref, tj_ref, q_ref, k_ref, v_ref, kmean_ref, bias_ref, o_ref,
                 qs_ref, sel_ref, m_ref, l_ref, acc_ref):
    t = pl.program_id(1)
    i = ti_ref[t]
    j = tj_ref[t]
    nb = kmean_ref.shape[1]
    head = lax.broadcasted_iota(jnp.int32, (1, 256), 1) // HEAD_DIM
    col = lax.broadcasted_iota(jnp.int32, (1, nb), 1)

    @pl.when(j == i)
    def _():
        q = q_ref[0]
        for h in range(MOBA_HEADS):
            qh = jnp.where(head == h, q, 0.0)
            qs_ref[h] = (qh * SCALE).astype(BF16)
            gate = _dot_nt(qh, kmean_ref[0], precision=HI)
            valid = col < i
            gate = jnp.where(valid, gate, NEG)
            sel = valid & (_rank_desc(gate, nb) < MOBA_TOPK)
            sel_ref[h] = jnp.where(sel, 1.0, 0.0)
        _init_state(m_ref, l_ref, acc_ref)

    kt = k_ref[0].astype(BF16)
    vt = v_ref[0].astype(BF16)
    tsel = jnp.minimum(i - j, 2)
    for h in range(MOBA_HEADS):
        picked = jnp.sum(jnp.where(col == j, sel_ref[h], 0.0), axis=-1, keepdims=True)
        ok = picked + jnp.where(j == i, 1.0, 0.0) > 0.5
        s = jnp.where(ok, _dot_nt(qs_ref[h], kt) + bias_ref[h, tsel], NEG)
        _softmax_step(s, vt, m_ref, l_ref, acc_ref, h)

    @pl.when(j == 0)
    def _():
        o = jnp.zeros((TQ, 256), F32)
        for h in range(MOBA_HEADS):
            o = jnp.where(head == h, acc_ref[h] / l_ref[h], o)
        o_ref[0] = o


def _moba_prompt(mq, m_row, kmean, bias_tiles):
    b, s, _ = mq.shape
    nb = s // MOBA_BLOCK
    ti, tj = _tri_tables(s // TQ)
    grid_spec = pltpu.PrefetchScalarGridSpec(
        num_scalar_prefetch=2,
        grid=(b, int(ti.shape[0])),
        in_specs=[
            pl.BlockSpec((1, TQ, 256), lambda b_, t, ti_, tj_: (b_, ti_[t], 0)),
            pl.BlockSpec((1, TQ, 256), lambda b_, t, ti_, tj_: (b_, tj_[t], 0)),
            pl.BlockSpec((1, TQ, 256), lambda b_, t, ti_, tj_: (b_, tj_[t], 1)),
            pl.BlockSpec((1, nb, 256), lambda b_, t, ti_, tj_: (b_, 0, 0)),
            pl.BlockSpec((4, 3, TQ, TQ), lambda b_, t, ti_, tj_: (0, 0, 0, 0)),
        ],
        out_specs=pl.BlockSpec((1, TQ, 256), lambda b_, t, ti_, tj_: (b_, ti_[t], 0)),
        scratch_shapes=[
            pltpu.VMEM((4, TQ, 256), BF16),
            pltpu.VMEM((4, TQ, nb), F32),
            pltpu.VMEM((4, TQ, 1), F32),
            pltpu.VMEM((4, TQ, 1), F32),
            pltpu.VMEM((4, TQ, 256), F32),
        ],
    )
    return pl.pallas_call(
        _moba_kernel,
        grid_spec=grid_spec,
        out_shape=jax.ShapeDtypeStruct((b, s, 256), F32),
        compiler_params=_cparams(("parallel", "arbitrary")),
        name="moba_prompt",
    )(ti, tj, mq, m_row, m_row, kmean, bias_tiles)


def _compress_kernel(r_ref, w1_ref, pos_ref, w1raw_ref, w2_ref, gkc_ref, kc_ref, vc_ref):
    a = jnp.dot(r_ref[0], w1_ref[...], preferred_element_type=F32, precision=HI)
    n = a.shape[0]
    outs = (kc_ref, vc_ref)
    for kv in range(2):
        pos = jnp.broadcast_to(pos_ref[kv], (8, pos_ref.shape[-1]))
        cst = jnp.dot(pos, w1raw_ref[kv], preferred_element_type=F32, precision=HI)[0:1]
        top = a[:, 256 * kv:256 * kv + 128]
        bot = a[:, 256 * kv + 128:256 * kv + 256]
        hid = top + pltpu.roll(bot, n - 1, 0) + cst
        out = jnp.dot(jax.nn.gelu(hid), w2_ref[kv], preferred_element_type=F32, precision=HI)
        if kv == 0:
            out = out * lax.rsqrt(jnp.mean(out * out, axis=-1, keepdims=True) + EPS) * gkc_ref[...]
        outs[kv][0] = out


def _nsa_compress(ckv, w1cat, pos_flat, w1raw, w2x4, gkc4):
    b, s, _ = ckv.shape
    n = s // NSA_CMP_STRIDE
    r = ckv.reshape(b, n, NSA_CMP_STRIDE * 128)
    return pl.pallas_call(
        _compress_kernel,
        grid=(b,),
        in_specs=[
            pl.BlockSpec((1, n, NSA_CMP_STRIDE * 128), lambda i: (i, 0, 0)),
            pl.BlockSpec((NSA_CMP_STRIDE * 128, 512), lambda i: (0, 0)),
            pl.BlockSpec((2, 1, NSA_CMP_BLOCK * HEAD_DIM), lambda i: (0, 0, 0)),
            pl.BlockSpec((2, NSA_CMP_BLOCK * HEAD_DIM, NSA_CMP_HIDDEN), lambda i: (0, 0, 0)),
            pl.BlockSpec((2, NSA_CMP_HIDDEN, 256), lambda i: (0, 0, 0)),
            pl.BlockSpec((1, 256), lambda i: (0, 0)),
        ],
        out_specs=[pl.BlockSpec((1, n, 256), lambda i: (i, 0, 0))] * 2,
        out_shape=[jax.ShapeDtypeStruct((b, n, 256), F32)] * 2,
        compiler_params=_cparams(("parallel",)),
        name="nsa_compress",
    )(r, w1cat, pos_flat, w1raw, w2x4, gkc4)


def _nsa_cmp_kernel(q_ref, kc_ref, vc_ref, ovl_ref, o_ref, sel_ref):
    i = pl.program_id(1)
    ncmp = kc_ref.shape[1]
    nsel = ovl_ref.shape[1]
    q = q_ref[0]
    head = lax.broadcasted_iota(jnp.int32, (1, 256), 1) // HEAD_DIM
    tpos = i * TQ + lax.broadcasted_iota(jnp.int32, (TQ, 1), 0)
    blk_end = lax.broadcasted_iota(jnp.int32, (1, ncmp), 1) * NSA_CMP_STRIDE + (NSA_CMP_BLOCK - 1)
    valid = blk_end <= tpos
    vc = vc_ref[0].astype(BF16)
    imp = jnp.zeros((TQ, nsel), F32)
    o = jnp.zeros((TQ, 256), F32)
    for h in range(NSA_HEADS):
        qh = jnp.where(head == h, q, 0.0)
        s = jnp.where(valid, _dot_nt(qh, kc_ref[0], precision=HI) * SCALE, NEG)
        p = jnp.where(valid, jnp.exp(s - jnp.max(s, axis=-1, keepdims=True)), 0.0)
        p = p / jnp.maximum(jnp.sum(p, axis=-1, keepdims=True), 1e-30)
        o = jnp.where(head == h, jnp.dot(p.astype(BF16), vc, preferred_element_type=F32), o)
        imp = imp + jnp.dot(p, ovl_ref[...], preferred_element_type=F32, precision=HI)
    o_ref[0] = o
    jcol = lax.broadcasted_iota(jnp.int32, (1, nsel), 1)
    cur = tpos // NSA_SEL_BLOCK
    forced = (jcol == 0) | (jcol == cur) | (jcol == cur - 1)
    score = jnp.where(jcol <= cur, imp + jnp.where(forced, FORCE_BONUS, 0.0), NEG)
    sel = (jcol <= cur) & (_rank_desc(score, nsel) < NSA_SEL_TOPK)
    sel_ref[0] = jnp.where(sel, 1.0, 0.0)


def _nsa_cmp_select(nq, kc4, vc4, overlap):
    b, s, _ = nq.shape
    ncmp = kc4.shape[1]
    nsel = overlap.shape[1]
    return pl.pallas_call(
        _nsa_cmp_kernel,
        grid=(b, s // TQ),
        in_specs=[
            pl.BlockSpec((1, TQ, 256), lambda b_, i: (b_, i, 0)),
            pl.BlockSpec((1, ncmp, 256), lambda b_, i: (b_, 0, 0)),
            pl.BlockSpec((1, ncmp, 256), lambda b_, i: (b_, 0, 0)),
            pl.BlockSpec((ncmp, nsel), lambda b_, i: (0, 0)),
        ],
        out_specs=[pl.BlockSpec((1, TQ, 256), lambda b_, i: (b_, i, 0)),
                   pl.BlockSpec((1, TQ, nsel), lambda b_, i: (b_, i, 0))],
        out_shape=[jax.ShapeDtypeStruct((b, s, 256), F32), jax.ShapeDtypeStruct((b, s, nsel), F32)],
        compiler_params=_cparams(("parallel", "parallel")),
        name="nsa_cmp_select",
    )(nq, kc4, vc4, overlap)


def _load_q128(q_ref, qs_ref):
    q = q_ref[0] * SCALE
    low = lax.broadcasted_iota(jnp.int32, (1, 128), 1) < HEAD_DIM
    for h in range(NSA_HEADS):
        slab = q[:, 128 * (h // 2):128 * (h // 2) + 128]
        if h % 2:
            slab = pltpu.roll(slab, HEAD_DIM, 1)
        qs_ref[h] = jnp.where(low, slab, 0.0).astype(BF16)


def _store_o128(acc_ref, l_ref, o_ref):
    low = lax.broadcasted_iota(jnp.int32, (1, 128), 1) < HEAD_DIM
    for pair in range(NSA_HEADS // 2):
        even = pltpu.roll(acc_ref[2 * pair] / l_ref[2 * pair], HEAD_DIM, 1)
        odd = acc_ref[2 * pair + 1] / l_ref[2 * pair + 1]
        o_ref[0, :, 128 * pair:128 * pair + 128] = jnp.where(low, even, odd)


def _nsa_sel_kernel(ti_ref, tj_ref, q_ref, kv_ref, selm_ref, exp_ref, bias_ref, o_ref,
                    qs_ref, m_ref, l_ref, acc_ref):
    t = pl.program_id(1)
    i = ti_ref[t]
    j = tj_ref[t]

    @pl.when(j == i)
    def _():
        _load_q128(q_ref, qs_ref)
        _init_state(m_ref, l_ref, acc_ref)

    kv = kv_ref[0].astype(BF16)
    picked = jnp.dot(selm_ref[0].astype(BF16), exp_ref[...], preferred_element_type=F32) > 0.5
    tsel = jnp.minimum(i - j, 2)
    for h in range(NSA_HEADS):
        s = jnp.where(picked, _dot_nt(qs_ref[h], kv) + bias_ref[h, tsel], NEG)
        _softmax_step(s, kv, m_ref, l_ref, acc_ref, h)

    @pl.when(j == 0)
    def _():
        _store_o128(acc_ref, l_ref, o_ref)


def _nsa_win_kernel(ti_ref, tj_ref, q_ref, kv_ref, bias_ref, o_ref, qs_ref, m_ref, l_ref, acc_ref):
    t = pl.program_id(1)
    i = ti_ref[t]
    j = tj_ref[t]

    @pl.when(j == i)
    def _():
        _load_q128(q_ref, qs_ref)
        _init_state(m_ref, l_ref, acc_ref)

    kv = kv_ref[0].astype(BF16)
    dist = ((i - j) * TQ + lax.broadcasted_iota(jnp.int32, (TQ, TQ), 0)
            - lax.broadcasted_iota(jnp.int32, (TQ, TQ), 1))
    inwin = dist < NSA_WINDOW
    tsel = jnp.minimum(i - j, 2)
    for h in range(NSA_HEADS):
        s = jnp.where(inwin, _dot_nt(qs_ref[h], kv) + bias_ref[h, tsel], NEG)
        _softmax_step(s, kv, m_ref, l_ref, acc_ref, h)

    @pl.when((j == 0) | (i - j == NSA_WINDOW // TQ))
    def _():
        _store_o128(acc_ref, l_ref, o_ref)


def _kv128_scratch():
    return [
        pltpu.VMEM((4, TQ, 128), BF16),
        pltpu.VMEM((4, TQ, 1), F32),
        pltpu.VMEM((4, TQ, 1), F32),
        pltpu.VMEM((4, TQ, 128), F32),
    ]


def _nsa_selected(nq, n_row, selmask, expand, bias_tiles):
    b, s, _ = nq.shape
    nsel = selmask.shape[-1]
    ti, tj = _tri_tables(s // TQ)
    grid_spec = pltpu.PrefetchScalarGridSpec(
        num_scalar_prefetch=2,
        grid=(b, int(ti.shape[0])),
        in_specs=[
            pl.BlockSpec((1, TQ, 256), lambda b_, t, ti_, tj_: (b_, ti_[t], 0)),
            pl.BlockSpec((1, TQ, 128), lambda b_, t, ti_, tj_: (b_, tj_[t], 1)),
            pl.BlockSpec((1, TQ, nsel), lambda b_, t, ti_, tj_: (b_, ti_[t], 0)),
            pl.BlockSpec((nsel, TQ), lambda b_, t, ti_, tj_: (0, tj_[t])),
            pl.BlockSpec((4, 3, TQ, TQ), lambda b_, t, ti_, tj_: (2, 0, 0, 0)),
        ],
        out_specs=pl.BlockSpec((1, TQ, 256), lambda b_, t, ti_, tj_: (b_, ti_[t], 0)),
        scratch_shapes=_kv128_scratch(),
    )
    return pl.pallas_call(
        _nsa_sel_kernel,
        grid_spec=grid_spec,
        out_shape=jax.ShapeDtypeStruct((b, s, 256), F32),
        compiler_params=_cparams(("parallel", "arbitrary")),
        name="nsa_selected",
    )(ti, tj, nq, n_row, selmask, expand, bias_tiles)


def _nsa_window(nq, w_row, bias_tiles):
    b, s, _ = nq.shape
    ti, tj = _tri_tables(s // TQ, depth=NSA_WINDOW // TQ + 1)
    grid_spec = pltpu.PrefetchScalarGridSpec(
        num_scalar_prefetch=2,
        grid=(b, int(ti.shape[0])),
        in_specs=[
            pl.BlockSpec((1, TQ, 256), lambda b_, t, ti_, tj_: (b_, ti_[t], 0)),
            pl.BlockSpec((1, TQ, 128), lambda b_, t, ti_, tj_: (b_, tj_[t], 0)),
            pl.BlockSpec((4, 3, TQ, TQ), lambda b_, t, ti_, tj_: (2, 0, 0, 0)),
        ],
        out_specs=pl.BlockSpec((1, TQ, 256), lambda b_, t, ti_, tj_: (b_, ti_[t], 0)),
        scratch_shapes=_kv128_scratch(),
    )
    return pl.pallas_call(
        _nsa_win_kernel,
        grid_spec=grid_spec,
        out_shape=jax.ShapeDtypeStruct((b, s, 256), F32),
        compiler_params=_cparams(("parallel", "arbitrary")),
        name="nsa_window",
    )(ti, tj, nq, w_row, bias_tiles)


def _merge_kernel(x_ref, g_ref, oa_ref, ob_ref, ocmp_ref, osel_ref, owin_ref, ngate_ref,
                  wbg_ref, wa_ref, wb_ref, wc_ref, wout_ref, gexp_ref, o_ref):
    x = x_ref[...]
    u = _rms(x, g_ref[...]).astype(BF16)
    ng = jax.nn.sigmoid(ngate_ref[...])
    gx = jnp.dot(ng, gexp_ref[...], preferred_element_type=F32, precision=HI)
    oc = gx[:, 0:256] * ocmp_ref[...] + gx[:, 256:512] * osel_ref[...] + gx[:, 512:768] * owin_ref[...]
    branches = ((oa_ref[...], wa_ref), (ob_ref[...], wb_ref), (oc, wc_ref))
    hsum = jnp.zeros(x.shape, F32)
    for k, (ob, w_ref) in enumerate(branches):
        gate = jax.nn.sigmoid(jnp.dot(u, wbg_ref[:, D_MODEL * k:D_MODEL * (k + 1)], preferred_element_type=F32))
        hsum = hsum + gate * jnp.dot(ob.astype(BF16), w_ref[...], preferred_element_type=F32)
    o_ref[...] = x + jnp.dot(hsum.astype(BF16), wout_ref[...], preferred_element_type=F32)


def _merge(x, g, o_a, o_b, o_cmp, o_sel, o_win, ngate, wbg, wa, wb, wc, wout, gexp, tm):
    n = x.shape[0]
    row = lambda w: pl.BlockSpec((tm, w), lambda i: (i, 0))
    full = lambda a: pl.BlockSpec(a.shape, lambda i: (0,) * a.ndim)
    return pl.pallas_call(
        _merge_kernel,
        grid=(n // tm,),
        in_specs=[row(D_MODEL), full(g), row(256), row(512), row(256), row(256), row(256), row(128),
                  full(wbg), full(wa), full(wb), full(wc), full(wout), full(gexp)],
        out_specs=row(D_MODEL),
        out_shape=jax.ShapeDtypeStruct((n, D_MODEL), F32),
        compiler_params=_cparams(("parallel",)),
        name="merge",
    )(x, g, o_a, o_b, o_cmp, o_sel, o_win, ngate, wbg, wa, wb, wc, wout, gexp)


def _rel_bucket(dist):
    n = jnp.maximum(dist, 0)
    exact = N_BUCKETS // 2
    nf = jnp.maximum(n, 1).astype(F32)
    large = exact + (jnp.log(nf / exact) / math.log(MAX_DISTANCE / exact) * (N_BUCKETS - exact)).astype(jnp.int32)
    return jnp.where(n < exact, n, jnp.minimum(large, N_BUCKETS - 1))


def _head_bias(table, dist):
    hidx = jnp.arange(table.shape[1]).reshape(-1, 1, 1)
    return table.T[hidx, _rel_bucket(dist)].astype(F32)


def _rms_norm(x, g):
    xf = x.astype(F32)
    y = xf * lax.rsqrt(jnp.mean(xf * xf, axis=-1, keepdims=True) + EPS)
    return (y * g.astype(F32)).astype(x.dtype)


def _masked_softmax(logits, mask):
    logits = jnp.where(mask, logits, NEG)
    m = jnp.max(logits, axis=-1, keepdims=True)
    p = jnp.where(mask, jnp.exp(logits - m), 0.0)
    return p / jnp.maximum(jnp.sum(p, axis=-1, keepdims=True), 1e-30)


def _gather_rows(pool_l, page_table, new_rows, pos, head=None):
    b = jnp.arange(pos.shape[0]).reshape((-1,) + (1,) * (pos.ndim - 1))
    pc = jnp.clip(pos, 0, PAST_LEN - 1)
    phys = page_table[b, pc // PAGE_SIZE]
    pn = jnp.clip(pos - PAST_LEN, 0, new_rows.shape[1] - 1)
    if head is None:
        past = pool_l[phys, pc % PAGE_SIZE]
        new = new_rows[b, pn]
    else:
        past = pool_l[phys, pc % PAGE_SIZE, :, head]
        new = new_rows[b, pn, :, head]
    is_past = (pos < PAST_LEN).reshape(pos.shape + (1,) * (past.ndim - pos.ndim))
    return jnp.where(is_past, past, new)


def _block_means(k, nb):
    bsz, length, h, d = k.shape
    kb = jnp.pad(k, ((0, 0), (0, nb * MOBA_BLOCK - length), (0, 0), (0, 0))).reshape(bsz, nb, MOBA_BLOCK, h, d)
    return jnp.mean(kb.astype(F32), axis=2), kb


def _moba_select(q, kmean, own):
    nb = kmean.shape[1]
    gate = jnp.einsum('bqhd,bnhd->bhqn', q, kmean, preferred_element_type=F32)
    gate = jnp.where(jnp.arange(nb)[None, :] < own[:, None], gate, NEG)
    _, sel = lax.top_k(gate, min(MOBA_TOPK, nb))
    return sel


def _moba_decode(q, m_new, pool_l, page_table, table):
    db, qn, h, d = q.shape
    length = PAST_LEN + qn
    t = PAST_LEN + jnp.arange(qn)
    k_past = pool_l[page_table, :, 0].reshape(db, PAST_LEN, h, d)
    kmean, _ = _block_means(jnp.concatenate([k_past, m_new[:, :, 0]], axis=1), -(-length // MOBA_BLOCK))
    own = t // MOBA_BLOCK
    sel = _moba_select(q, kmean, own)
    n = sel.shape[-1]
    ar = jnp.arange(MOBA_BLOCK)
    pos_sel = (sel[..., None] * MOBA_BLOCK + ar).reshape(db, h, qn, n * MOBA_BLOCK)
    ok_sel = jnp.broadcast_to((sel < own[:, None])[..., None], sel.shape + (MOBA_BLOCK,)).reshape(pos_sel.shape)
    pos_own = own[:, None] * MOBA_BLOCK + ar
    pos = jnp.concatenate([pos_sel, jnp.broadcast_to(pos_own, (db, h, qn, MOBA_BLOCK))], -1)
    mask = jnp.concatenate([ok_sel, jnp.broadcast_to(pos_own <= t[:, None], (db, h, qn, MOBA_BLOCK))], -1)
    kv = _gather_rows(pool_l, page_table, m_new, pos, jnp.arange(h).reshape(1, h, 1, 1))
    logits = jnp.einsum('bqhd,bhqkd->bhqk', q, kv[..., 0, :], preferred_element_type=F32) * SCALE
    p = _masked_softmax(logits + _head_bias(table, t[:, None] - pos), mask)
    o = jnp.einsum('bhqk,bhqkd->bqhd', p, kv[..., 1, :])
    return o.reshape(db, qn, h * d)


def _diff_combine(om, lam, lam_init, subln):
    lamf = lam.astype(F32)
    lam_full = jnp.exp(jnp.sum(lamf[0] * lamf[1])) - jnp.exp(jnp.sum(lamf[2] * lamf[3])) + lam_init
    o = om[..., 0, :] - lam_full * om[..., 1, :]
    o = _rms_norm(o, subln) * (1.0 - lam_init)
    return o.reshape(o.shape[:-2] + (-1,))


def _diff_decode(q, d_row, pool_l, page_table, table, lam, lam_init, subln):
    db, qn, h, _, d = q.shape
    t = PAST_LEN + jnp.arange(qn)

    def update(carry, k, v, pos):
        m, s, acc = carry
        logits = jnp.einsum('bqhmd,bkhmd->bhmqk', q, k, preferred_element_type=F32) * SCALE
        logits = logits + _head_bias(table, t[:, None] - pos[None, :])[:, None]
        mask = pos[None, :] <= t[:, None]
        logits = jnp.where(mask, logits, NEG)
        m_new = jnp.maximum(m, jnp.max(logits, axis=-1))
        p = jnp.where(mask, jnp.exp(logits - m_new[..., None]), 0.0)
        corr = jnp.exp(m - m_new)
        s = s * corr + jnp.sum(p, axis=-1)
        acc = acc * corr[..., None] + jnp.einsum('bhmqk,bkhe->bhmqe', p, v, preferred_element_type=F32)
        return (m_new, s, acc)

    def page_step(carry, xs_):
        j, phys = xs_
        kv = pool_l[phys]
        k = kv[:, :, 0].reshape(db, PAGE_SIZE, h, 2, d)
        return update(carry, k, kv[:, :, 1], j * PAGE_SIZE + jnp.arange(PAGE_SIZE)), None

    init = (jnp.full((db, h, 2, qn), NEG, F32), jnp.zeros((db, h, 2, qn), F32),
            jnp.zeros((db, h, 2, qn, DIFF_VDIM), F32))
    carry, _ = lax.scan(page_step, init, (jnp.arange(page_table.shape[1]), page_table.T))
    _, s, acc = update(carry, d_row[:, :, 0].reshape(db, qn, h, 2, d), d_row[:, :, 1], t)
    om = jnp.moveaxis(acc / s[..., None], 3, 1)
    return _diff_combine(om, lam, lam_init, subln)


def _compress(rows, pos_emb, w1, w2):
    bsz, length, d = rows.shape
    n_cmp = (length - NSA_CMP_BLOCK) // NSA_CMP_STRIDE + 1
    idx = jnp.arange(n_cmp)[:, None] * NSA_CMP_STRIDE + jnp.arange(NSA_CMP_BLOCK)[None, :]
    blk = (rows[:, idx] + pos_emb).reshape(bsz, n_cmp, NSA_CMP_BLOCK * d)
    return jax.nn.gelu(blk @ w1) @ w2


def _nsa_compressed(q, ck, cv, t, g_kc, cmp_pos, cmp_w1, cmp_w2):
    kc = _rms_norm(_compress(ck, cmp_pos[0], cmp_w1[0], cmp_w2[0]), g_kc)
    vc = _compress(cv, cmp_pos[1], cmp_w1[1], cmp_w2[1])
    starts = jnp.arange(kc.shape[1]) * NSA_CMP_STRIDE
    logits = jnp.einsum('bqhd,bnd->bhqn', q, kc, preferred_element_type=F32) * SCALE
    p = _masked_softmax(logits, (starts + NSA_CMP_BLOCK - 1)[None, :] <= t[:, None])
    o = jnp.einsum('bhqn,bnd->bqhd', p, vc)
    lo = jnp.arange(-(-ck.shape[1] // NSA_SEL_BLOCK)) * NSA_SEL_BLOCK
    overlap = ((starts[:, None] < lo[None, :] + NSA_SEL_BLOCK)
               & (starts[:, None] + NSA_CMP_BLOCK > lo[None, :])).astype(F32)
    imp = jnp.einsum('bhqn,nj->bqj', p, overlap)
    return o, imp


def _nsa_select_blocks(imp, t):
    j = jnp.arange(imp.shape[-1])[None, :]
    cur = (t // NSA_SEL_BLOCK)[:, None]
    forced = (j == 0) | (j == cur) | (j == cur - 1)
    score = jnp.where(j <= cur, imp + FORCE_BONUS * forced, NEG)
    _, idx = lax.top_k(score, min(NSA_SEL_TOPK, imp.shape[-1]))
    return idx


def _nsa_sel_attend(q, ks, vs, pos, t, table):
    dist = (t[None, :, None] - pos)[:, None]
    logits = jnp.einsum('bqhd,bqkd->bhqk', q, ks, preferred_element_type=F32) * SCALE
    p = _masked_softmax(logits + _head_bias(table, dist), dist >= 0)
    return jnp.einsum('bhqk,bqkd->bqhd', p, vs)


def _nsa_window_decode(q, buf, w_new, t, table):
    w_buf = buf.shape[1]
    kv = jnp.concatenate([buf, w_new], axis=1)
    pos = PAST_LEN - w_buf + jnp.arange(kv.shape[1])
    dist = t[:, None] - pos[None, :]
    logits = jnp.einsum('bqhd,bkd->bhqk', q, kv[:, :, 0], preferred_element_type=F32) * SCALE
    p = _masked_softmax(logits + _head_bias(table, dist), (dist >= 0) & (dist < NSA_WINDOW))
    o = jnp.einsum('bhqk,bkd->bqhd', p, kv[:, :, 1])
    return o, kv[:, kv.shape[1] - w_buf:]


def _nsa_decode(q, n_new, w_new, pool_l, page_table, buf, table, g_kc, cmp_pos, cmp_w1, cmp_w2):
    db, qn, h, d = q.shape
    t = PAST_LEN + jnp.arange(qn)
    past_c = pool_l[page_table, :, :2].reshape(db, PAST_LEN, 2, d)
    all_c = jnp.concatenate([past_c, n_new[:, :, :2]], axis=1)
    o_cmp, imp = _nsa_compressed(q, all_c[:, :, 0], all_c[:, :, 1], t, g_kc, cmp_pos, cmp_w1, cmp_w2)
    idx = _nsa_select_blocks(imp, t)
    pos = (idx[..., None] * NSA_SEL_BLOCK + jnp.arange(NSA_SEL_BLOCK)).reshape(db, qn, -1)
    rows = _gather_rows(pool_l, page_table, n_new, pos)
    o_sel = _nsa_sel_attend(q, rows[..., 2, :], rows[..., 3, :], pos, t, table)
    o_win, new_buf = _nsa_window_decode(q, buf, w_new, t, table)
    return o_cmp, o_sel, o_win, new_buf


def _pack_layer(l, norm_gain, ffn_w1, ffn_w2, w_in, qk_gain, cmp_pos, cmp_w1, cmp_w2,
                w_branch_moba, w_branch_diff, w_branch_nsa, w_out):
    w = w_in[l]
    w_pack = jnp.concatenate(
        [w[:, :N_MAIN], jnp.pad(w[:, N_MAIN:N_MAIN + 12], ((0, 0), (0, 116)))], axis=1).astype(BF16)
    g = qk_gain[l]
    ones = jnp.ones((HEAD_DIM,), F32)
    zeros = jnp.zeros((HEAD_DIM,), F32)

    def rep(v, k):
        return jnp.tile(v, k)

    gain_row = jnp.concatenate([
        rep(g[0], 4), rep(g[1], 4), rep(ones, 4), rep(g[2], 8), rep(g[3], 8), rep(ones, 8), rep(g[4], 4),
        ones, ones, g[6], ones, g[7], ones])[None, :]
    nmask_row = jnp.concatenate([
        rep(ones, 8), rep(zeros, 4), rep(ones, 16), rep(zeros, 8), rep(ones, 4),
        zeros, zeros, ones, zeros, ones, zeros])[None, :]
    w1 = cmp_w1[l].reshape(2, 2, NSA_CMP_STRIDE, HEAD_DIM, NSA_CMP_HIDDEN)
    zero = jnp.zeros((NSA_CMP_STRIDE, HEAD_DIM, NSA_CMP_HIDDEN), F32)
    k_cols = jnp.concatenate([w1[0, 0], w1[0, 1], zero, zero], axis=-1)
    v_cols = jnp.concatenate([zero, zero, w1[1, 0], w1[1, 1]], axis=-1)
    w1cat = jnp.concatenate([k_cols, v_cols], axis=1).reshape(NSA_CMP_STRIDE * 128, 512)
    return dict(
        g0=norm_gain[l, 0][None, :], g1=norm_gain[l, 1][None, :], g2=norm_gain[l, 2][None, :],
        ffn1=(ffn_w1[l, 0].astype(BF16), ffn_w2[l, 0].astype(BF16)),
        ffn2=(ffn_w1[l, 1].astype(BF16), ffn_w2[l, 1].astype(BF16)),
        w_pack=w_pack, gain_row=gain_row, nmask_row=nmask_row,
        wbg=w[:, N_MAIN + 12:].astype(BF16),
        w1cat=w1cat, pos_flat=cmp_pos[l].reshape(2, 1, NSA_CMP_BLOCK * HEAD_DIM), w1raw=cmp_w1[l],
        w2x4=jnp.tile(cmp_w2[l], (1, 1, 4)), gkc4=jnp.tile(g[5], 4)[None, :],
        wa=w_branch_moba[l].astype(BF16), wb=w_branch_diff[l].astype(BF16), wc=w_branch_nsa[l].astype(BF16),
        wout=w_out[l].astype(BF16),
    )


def _bias_tiles(rel_bias):
    bvec = rel_bias.T[:, _rel_bucket(jnp.arange(3 * TQ))]
    qi = jnp.arange(TQ)[:, None]
    ki = jnp.arange(TQ)[None, :]
    diag = jnp.where(qi >= ki, bvec[:, jnp.maximum(qi - ki, 0)], NEG)
    prev = bvec[:, TQ + qi - ki]
    far = bvec[:, 2 * TQ + qi - ki]
    return jnp.stack([diag, prev, far], axis=1).astype(F32)


def _const_tables(seq):
    blocks = np.arange(seq // NSA_CMP_STRIDE) * NSA_CMP_STRIDE
    lo = np.arange(seq // NSA_SEL_BLOCK) * NSA_SEL_BLOCK
    overlap = ((blocks[:, None] < lo[None, :] + NSA_SEL_BLOCK) & (blocks[:, None] + NSA_CMP_BLOCK > lo[None, :]))
    expand = np.arange(seq // NSA_SEL_BLOCK)[:, None] == (np.arange(seq)[None, :] // NSA_SEL_BLOCK)
    gexp = np.zeros((128, 768), np.float32)
    for k in range(3):
        for h in range(NSA_HEADS):
            gexp[k * NSA_HEADS + h, 256 * k + HEAD_DIM * h:256 * k + HEAD_DIM * (h + 1)] = 1.0
    bd = np.kron(np.eye(4, dtype=np.float32), np.ones((HEAD_DIM, HEAD_DIM), np.float32))
    return (jnp.asarray(overlap, F32), jnp.asarray(expand, BF16), jnp.asarray(gexp), jnp.asarray(bd, BF16))


def kernel(x_prompt, x_sample, cache_moba_kv, cache_diff_kv, cache_nsa_kv, state_nsa_win, page_table, rel_bias, norm_gain, ffn_w1, ffn_w2, w_in, qk_gain, diff_lambda, diff_subln, cmp_pos, cmp_w1, cmp_w2, w_branch_moba, w_branch_diff, w_branch_nsa, w_out):
    bsz, seq, _ = x_prompt.shape
    db, dq_len, _ = x_sample.shape
    n_p = bsz * seq
    n_s = db * dq_len
    tab_m = rel_bias[:, :MOBA_HEADS]
    tab_d = rel_bias[:, MOBA_HEADS:MOBA_HEADS + DIFF_HEADS]
    tab_n = rel_bias[:, MOBA_HEADS + DIFF_HEADS:]
    bias_tiles = _bias_tiles(rel_bias)
    overlap, expand, gexp, bd = _const_tables(seq)

    xp = x_prompt.reshape(n_p, D_MODEL)
    xs = x_sample.reshape(n_s, D_MODEL)
    outs = [[] for _ in range(8)]
    for l in range(DEPTH):
        lam_init = 0.8 - 0.6 * math.exp(-0.3 * l)
        pk = _pack_layer(l, norm_gain, ffn_w1, ffn_w2, w_in, qk_gain, cmp_pos, cmp_w1, cmp_w2,
                         w_branch_moba, w_branch_diff, w_branch_nsa, w_out)
        xp = _ffn(xp, pk['g0'], *pk['ffn1'], tm=1024)
        xs = _ffn(xs, pk['g0'], *pk['ffn1'], tm=n_s)

        mq, m_row, dq, d_row, nq, n_row, w_row, ngate = _inproj(
            xp, pk['g1'], pk['w_pack'], pk['gain_row'], pk['nmask_row'], bd, tm=256)
        r3 = lambda a: a.reshape(bsz, seq, a.shape[-1])
        mq, m_row, dq, d_row, nq, n_row, w_row = map(r3, (mq, m_row, dq, d_row, nq, n_row, w_row))
        o_a = _moba_prompt(mq, m_row, _moba_kmean(m_row), bias_tiles)
        o_b = _diff_prompt(dq, d_row, bias_tiles, diff_lambda[l], diff_subln[l][None, :], lam_init)
        kc4, vc4 = _nsa_compress(n_row[:, :, :128], pk['w1cat'], pk['pos_flat'], pk['w1raw'], pk['w2x4'], pk['gkc4'])
        o_cmp, selmask = _nsa_cmp_select(nq, kc4, vc4, overlap)
        o_sel = _nsa_selected(nq, n_row, selmask, expand, bias_tiles)
        o_win = _nsa_window(nq, w_row, bias_tiles)
        f2 = lambda a: a.reshape(n_p, a.shape[-1])
        xp = _merge(xp, pk['g1'], f2(o_a), f2(o_b), f2(o_cmp), f2(o_sel), f2(o_win), ngate,
                    pk['wbg'], pk['wa'], pk['wb'], pk['wc'], pk['wout'], gexp, tm=256)
        outs[0].append(m_row.reshape(bsz, seq, 2, MOBA_HEADS, HEAD_DIM))
        outs[2].append(d_row.reshape(bsz, seq, 2, DIFF_HEADS, DIFF_VDIM))
        outs[4].append(n_row.reshape(bsz, seq, 4, HEAD_DIM))
        outs[6].append(w_row[:, seq - min(NSA_WINDOW, seq):].reshape(bsz, min(NSA_WINDOW, seq), 2, HEAD_DIM))

        mq, m_row, dq, d_row, nq, n_row, w_row, ngate = _inproj(
            xs, pk['g1'], pk['w_pack'], pk['gain_row'], pk['nmask_row'], bd, tm=n_s)
        m_row = m_row.reshape(db, dq_len, 2, MOBA_HEADS, HEAD_DIM)
        d_row = d_row.reshape(db, dq_len, 2, DIFF_HEADS, DIFF_VDIM)
        n_row = n_row.reshape(db, dq_len, 4, HEAD_DIM)
        w_row = w_row.reshape(db, dq_len, 2, HEAD_DIM)
        o_a = _moba_decode(mq.reshape(db, dq_len, MOBA_HEADS, HEAD_DIM), m_row, cache_moba_kv[l], page_table, tab_m)
        o_b = _diff_decode(dq.reshape(db, dq_len, DIFF_HEADS, 2, HEAD_DIM), d_row, cache_diff_kv[l], page_table,
                           tab_d, diff_lambda[l], lam_init, diff_subln[l])
        o_cmp, o_sel, o_win, buf = _nsa_decode(
            nq.reshape(db, dq_len, NSA_HEADS, HEAD_DIM), n_row, w_row, cache_nsa_kv[l], page_table,
            state_nsa_win[l], tab_n, qk_gain[l, 5], cmp_pos[l], cmp_w1[l], cmp_w2[l])
        fs = lambda a: a.reshape(n_s, -1)
        xs = _merge(xs, pk['g1'], fs(o_a), fs(o_b), fs(o_cmp), fs(o_sel), fs(o_win), ngate,
                    pk['wbg'], pk['wa'], pk['wb'], pk['wc'], pk['wout'], gexp, tm=n_s)
        outs[1].append(m_row)
        outs[3].append(d_row)
        outs[5].append(n_row)
        outs[7].append(buf)

        xp = _ffn(xp, pk['g2'], *pk['ffn2'], tm=1024)
        xs = _ffn(xs, pk['g2'], *pk['ffn2'], tm=n_s)
    return (xp.reshape(bsz, seq, D_MODEL), xs.reshape(db, dq_len, D_MODEL)) + tuple(jnp.stack(o) for o in outs)
```

```python
import functools
import math

import numpy as np
import jax
import jax.numpy as jnp
from jax import lax
from jax.experimental import pallas as pl
from jax.experimental.pallas import tpu as pltpu

F32 = jnp.float32
BF16 = jnp.bfloat16
HI = lax.Precision.HIGHEST

D_MODEL = 1024
DEPTH = 2
PAST_LEN = 16384
PAGE_SIZE = 128
HEAD_DIM = 64
MOBA_HEADS = 4
MOBA_BLOCK = 256
MOBA_TOPK = 3
DIFF_HEADS = 4
DIFF_VDIM = 2 * HEAD_DIM
NSA_HEADS = 4
NSA_CMP_BLOCK = 32
NSA_CMP_STRIDE = 16
NSA_CMP_HIDDEN = 128
NSA_SEL_BLOCK = 64
NSA_SEL_TOPK = 16
NSA_WINDOW = 512
N_BUCKETS = 32
MAX_DISTANCE = 128
D_FF = 2688
EPS = 1e-6
NEG = -1e30
FORCE_BONUS = 1e4
SCALE = HEAD_DIM ** -0.5

TQ = 256
N_MAIN = 2944
N_PACK = 3072
VMEM_LIMIT = 56 * 1024 * 1024


def _cparams(sem):
    return pltpu.CompilerParams(dimension_semantics=sem, vmem_limit_bytes=VMEM_LIMIT)


def _rms(x, g):
    return x * lax.rsqrt(jnp.mean(x * x, axis=-1, keepdims=True) + EPS) * g


def _dot_nt(a, b, precision=None):
    return lax.dot_general(a, b, (((1,), (1,)), ((), ())), preferred_element_type=F32, precision=precision)


def _ffn_kernel(x_ref, g_ref, w1a_ref, w1b_ref, w2_ref, o_ref, u_scr, acc_scr):
    f = pl.program_id(1)

    @pl.when(f == 0)
    def _():
        u_scr[...] = _rms(x_ref[...], g_ref[...]).astype(BF16)
        acc_scr[...] = jnp.zeros_like(acc_scr)

    u = u_scr[...]
    a = jnp.dot(u, w1a_ref[...], preferred_element_type=F32)
    b = jnp.dot(u, w1b_ref[...], preferred_element_type=F32)
    h = (a * jax.nn.sigmoid(a) * b).astype(BF16)
    acc_scr[...] += jnp.dot(h, w2_ref[...], preferred_element_type=F32)

    @pl.when(f == pl.num_programs(1) - 1)
    def _():
        o_ref[...] = x_ref[...] + 0.5 * acc_scr[...]


def _ffn(x, g, w1, w2, tm):
    n = x.shape[0]
    tf = 896
    nf = D_FF // tf
    return pl.pallas_call(
        _ffn_kernel,
        grid=(n // tm, nf),
        in_specs=[
            pl.BlockSpec((tm, D_MODEL), lambda i, f: (i, 0)),
            pl.BlockSpec((1, D_MODEL), lambda i, f: (0, 0)),
            pl.BlockSpec((D_MODEL, tf), lambda i, f: (0, f)),
            pl.BlockSpec((D_MODEL, tf), lambda i, f: (0, f + nf)),
            pl.BlockSpec((tf, D_MODEL), lambda i, f: (f, 0)),
        ],
        out_specs=pl.BlockSpec((tm, D_MODEL), lambda i, f: (i, 0)),
        out_shape=jax.ShapeDtypeStruct((n, D_MODEL), F32),
        scratch_shapes=[pltpu.VMEM((tm, D_MODEL), BF16), pltpu.VMEM((tm, D_MODEL), F32)],
        compiler_params=_cparams(("parallel", "arbitrary")),
        name="ffn",
    )(x, g, w1, w1, w2)


_SLABS = (
    (0, 256, 0, 0),
    (256, 256, 1, 0),
    (512, 256, 1, 256),
    (768, 256, 2, 0),
    (1024, 256, 2, 256),
    (1280, 256, 3, 0),
    (1536, 256, 3, 256),
    (1792, 256, 3, 512),
    (2048, 256, 3, 768),
    (2304, 256, 4, 0),
    (2560, 256, 5, 0),
    (2816, 128, 6, 0),
    (2944, 128, 7, 0),
)


def _inproj_kernel(x_ref, g_ref, w_ref, gain_ref, nmask_ref, bd_ref,
                   mq_ref, mrow_ref, dq_ref, drow_ref, nq_ref, nrow_ref, wrow_ref, ngate_ref):
    outs = (mq_ref, mrow_ref, dq_ref, drow_ref, nq_ref, nrow_ref, wrow_ref, ngate_ref)
    u = _rms(x_ref[...], g_ref[...]).astype(BF16)
    for c0, w, oi, oc in _SLABS:
        z = jnp.dot(u, w_ref[:, c0:c0 + w], preferred_element_type=F32)
        if c0 < N_MAIN and c0 not in (512, 1792, 2048):
            ss = jnp.dot((z * z).astype(BF16), bd_ref[0:w, 0:w], preferred_element_type=F32)
            zn = z * lax.rsqrt(ss * (1.0 / HEAD_DIM) + EPS) * gain_ref[:, c0:c0 + w]
            z = jnp.where(nmask_ref[:, c0:c0 + w] > 0.5, zn, z)
        outs[oi][:, oc:oc + w] = z


def _inproj(x, g, w_pack, gain_row, nmask_row, bd, tm):
    n = x.shape[0]
    widths = (256, 512, 512, 1024, 256, 256, 128, 128)
    const = lambda i: (0, 0)
    return pl.pallas_call(
        _inproj_kernel,
        grid=(n // tm,),
        in_specs=[
            pl.BlockSpec((tm, D_MODEL), lambda i: (i, 0)),
            pl.BlockSpec((1, D_MODEL), const),
            pl.BlockSpec((D_MODEL, N_PACK), const),
            pl.BlockSpec((1, N_MAIN), const),
            pl.BlockSpec((1, N_MAIN), const),
            pl.BlockSpec((256, 256), const),
        ],
        out_specs=[pl.BlockSpec((tm, w), lambda i: (i, 0)) for w in widths],
        out_shape=[jax.ShapeDtypeStruct((n, w), F32) for w in widths],
        compiler_params=_cparams(("parallel",)),
        name="inproj",
    )(x, g, w_pack, gain_row, nmask_row, bd)


def _tri_tables(nq, depth=None):
    ti, tj = [], []
    for i in range(nq):
        lo = 0 if depth is None else max(0, i - depth + 1)
        for j in range(i, lo - 1, -1):
            ti.append(i)
            tj.append(j)
    return jnp.asarray(ti, jnp.int32), jnp.asarray(tj, jnp.int32)


def _softmax_update(s, m_ref, l_ref, narrow=True):
    m_old = m_ref[...]
    m_new = jnp.maximum(m_old, jnp.max(s, axis=-1, keepdims=True))
    alpha = jnp.exp(m_old - m_new)
    p = jnp.exp(s - m_new)
    l_ref[...] = alpha * l_ref[...] + jnp.sum(p, axis=-1, keepdims=True)
    m_ref[...] = m_new
    return alpha, (p.astype(BF16) if narrow else p)


def _rows(x, c, n=1):
    return x[TQ * c:TQ * (c + n)]


def _init_state(m_ref, l_ref, acc_ref):
    m_ref[...] = jnp.full(m_ref.shape, NEG, F32)
    l_ref[...] = jnp.zeros(l_ref.shape, F32)
    acc_ref[...] = jnp.zeros(acc_ref.shape, F32)


def _diff_kernel(ti_ref, tj_ref, q_ref, k_ref, v_ref, bias_ref, lam_ref, subln_ref, o_ref,
                 qs_ref, m_ref, l_ref, acc_ref, *, lam_init):
    t = pl.program_id(1)
    i = ti_ref[t]
    j = tj_ref[t]

    @pl.when(j == i)
    def _():
        q = q_ref[0] * SCALE
        chunk = lax.broadcasted_iota(jnp.int32, (1, 256), 1) // HEAD_DIM
        for c in range(8):
            half = q[:, 256 * (c // 4):256 * (c // 4) + 256]
            qs_ref[TQ * c:TQ * (c + 1)] = jnp.where(chunk == c % 4, half, 0.0).astype(BF16)
        _init_state(m_ref, l_ref, acc_ref)

    kt = k_ref[0].astype(BF16)
    vt = v_ref[0].astype(BF16)
    tsel = jnp.minimum(i - j, 2)
    logits = []
    for h in range(DIFF_HEADS):
        bias = bias_ref[h, tsel]
        sh = _dot_nt(_rows(qs_ref, 2 * h, 2), kt[:, 256 * (h // 2):256 * (h // 2) + 256])
        logits += [_rows(sh, 0) + bias, _rows(sh, 1) + bias]
    alpha, p = _softmax_update(jnp.concatenate(logits, axis=0), m_ref, l_ref)
    pv = [jnp.dot(_rows(p, 2 * h, 2), vt[:, DIFF_VDIM * h:DIFF_VDIM * (h + 1)], preferred_element_type=F32)
          for h in range(DIFF_HEADS)]
    acc_ref[...] = alpha * acc_ref[...] + jnp.concatenate(pv, axis=0)

    @pl.when(j == 0)
    def _():
        lam = lam_ref[...]
        lam_full = (jnp.exp(jnp.sum(lam[0:1] * lam[1:2], axis=-1, keepdims=True))
                    - jnp.exp(jnp.sum(lam[2:3] * lam[3:4], axis=-1, keepdims=True)) + lam_init)
        om = acc_ref[...] / l_ref[...]
        for h in range(DIFF_HEADS):
            o = _rms(_rows(om, 2 * h) - lam_full * _rows(om, 2 * h + 1), subln_ref[...]) * (1.0 - lam_init)
            o_ref[0, :, DIFF_VDIM * h:DIFF_VDIM * (h + 1)] = o


def _diff_prompt(dq, d_row, bias_tiles, lam, subln, lam_init):
    b, s, _ = dq.shape
    ti, tj = _tri_tables(s // TQ)
    grid_spec = pltpu.PrefetchScalarGridSpec(
        num_scalar_prefetch=2,
        grid=(b, int(ti.shape[0])),
        in_specs=[
            pl.BlockSpec((1, TQ, 512), lambda b_, t, ti_, tj_: (b_, ti_[t], 0)),
            pl.BlockSpec((1, TQ, 512), lambda b_, t, ti_, tj_: (b_, tj_[t], 0)),
            pl.BlockSpec((1, TQ, 512), lambda b_, t, ti_, tj_: (b_, tj_[t], 1)),
            pl.BlockSpec((4, 3, TQ, TQ), lambda b_, t, ti_, tj_: (1, 0, 0, 0)),
            pl.BlockSpec((4, HEAD_DIM), lambda b_, t, ti_, tj_: (0, 0)),
            pl.BlockSpec((1, DIFF_VDIM), lambda b_, t, ti_, tj_: (0, 0)),
        ],
        out_specs=pl.BlockSpec((1, TQ, 512), lambda b_, t, ti_, tj_: (b_, ti_[t], 0)),
        scratch_shapes=[
            pltpu.VMEM((8 * TQ, 256), BF16),
            pltpu.VMEM((8 * TQ, 1), F32),
            pltpu.VMEM((8 * TQ, 1), F32),
            pltpu.VMEM((8 * TQ, DIFF_VDIM), F32),
        ],
    )
    return pl.pallas_call(
        functools.partial(_diff_kernel, lam_init=lam_init),
        grid_spec=grid_spec,
        out_shape=jax.ShapeDtypeStruct((b, s, 512), F32),
        compiler_params=_cparams(("parallel", "arbitrary")),
        name="diff_prompt",
    )(ti, tj, dq, d_row, d_row, bias_tiles, lam, subln)


def _rank_desc(score, n):
    axis = score.ndim - 1
    col = lax.broadcasted_iota(jnp.int32, (1,) * axis + (score.shape[-1],), axis)
    rank = jnp.zeros(score.shape, F32)
    for jp in range(n):
        sj = score[..., jp:jp + 1]
        tie = jnp.where(jp < col, 1.0, 0.0)
        rank = rank + jnp.where(sj > score, 1.0, jnp.where(sj == score, tie, 0.0))
    return rank


def _kmean_kernel(k_ref, o_ref):
    nb = o_ref.shape[1]
    o_ref[0] = jnp.mean(k_ref[0].reshape(nb, MOBA_BLOCK, 256), axis=1)


def _moba_kmean(m_row):
    b, s, _ = m_row.shape
    nb = s // MOBA_BLOCK
    return pl.pallas_call(
        _kmean_kernel,
        grid=(b,),
        in_specs=[pl.BlockSpec((1, s, 256), lambda i: (i, 0, 0))],
        out_specs=pl.BlockSpec((1, nb, 256), lambda i: (i, 0, 0)),
        out_shape=jax.ShapeDtypeStruct((b, nb, 256), F32),
        compiler_params=_cparams(("parallel",)),
        name="moba_kmean",
    )(m_row)


def _moba_kernel(ti_ref, tj_ref, q_ref, k_ref, v_ref, kmean_ref, bias_ref, o_ref,
                 qs_ref, sel_ref, m_ref, l_ref, acc_ref):
    t = pl.program_id(1)
    i = ti_ref[t]
    j = tj_ref[t]
    nb = kmean_ref.shape[1]
    head = lax.broadcasted_iota(jnp.int32, (1, 256), 1) // HEAD_DIM
    col = lax.broadcasted_iota(jnp.int32, (1, nb), 1)

    @pl.when(j == i)
    def _():
        q = q_ref[0]
        for h in range(MOBA_HEADS):
            qh = jnp.where(head == h, q, 0.0)
            qs_ref[TQ * h:TQ * (h + 1)] = (qh * SCALE).astype(BF16)
            gate = _dot_nt(qh, kmean_ref[0], precision=HI)
            valid = col < i
            gate = jnp.where(valid, gate, NEG)
            sel = valid & (_rank_desc(gate, nb) < MOBA_TOPK)
            sel_ref[TQ * h:TQ * (h + 1)] = jnp.where(sel, 1.0, 0.0)
        _init_state(m_ref, l_ref, acc_ref)

    kt = k_ref[0].astype(BF16)
    vt = v_ref[0].astype(BF16)
    tsel = jnp.minimum(i - j, 2)
    bias = jnp.concatenate([bias_ref[h, tsel] for h in range(MOBA_HEADS)], axis=0)
    picked = jnp.sum(jnp.where(col == j, sel_ref[...], 0.0), axis=-1, keepdims=True)
    ok = picked + jnp.where(j == i, 1.0, 0.0) > 0.5
    s = jnp.where(ok, _dot_nt(qs_ref[...], kt) + bias, NEG)
    alpha, p = _softmax_update(s, m_ref, l_ref)
    acc_ref[...] = alpha * acc_ref[...] + jnp.dot(p, vt, preferred_element_type=F32)

    @pl.when(j == 0)
    def _():
        om = acc_ref[...] / l_ref[...]
        o = jnp.zeros((TQ, 256), F32)
        for h in range(MOBA_HEADS):
            o = jnp.where(head == h, _rows(om, h), o)
        o_ref[0] = o


def _moba_prompt(mq, m_row, kmean, bias_tiles):
    b, s, _ = mq.shape
    nb = s // MOBA_BLOCK
    ti, tj = _tri_tables(s // TQ)
    grid_spec = pltpu.PrefetchScalarGridSpec(
        num_scalar_prefetch=2,
        grid=(b, int(ti.shape[0])),
        in_specs=[
            pl.BlockSpec((1, TQ, 256), lambda b_, t, ti_, tj_: (b_, ti_[t], 0)),
            pl.BlockSpec((1, TQ, 256), lambda b_, t, ti_, tj_: (b_, tj_[t], 0)),
            pl.BlockSpec((1, TQ, 256), lambda b_, t, ti_, tj_: (b_, tj_[t], 1)),
            pl.BlockSpec((1, nb, 256), lambda b_, t, ti_, tj_: (b_, 0, 0)),
            pl.BlockSpec((4, 3, TQ, TQ), lambda b_, t, ti_, tj_: (0, 0, 0, 0)),
        ],
        out_specs=pl.BlockSpec((1, TQ, 256), lambda b_, t, ti_, tj_: (b_, ti_[t], 0)),
        scratch_shapes=[
            pltpu.VMEM((4 * TQ, 256), BF16),
            pltpu.VMEM((4 * TQ, nb), F32),
            pltpu.VMEM((4 * TQ, 1), F32),
            pltpu.VMEM((4 * TQ, 1), F32),
            pltpu.VMEM((4 * TQ, 256), F32),
        ],
    )
    return pl.pallas_call(
        _moba_kernel,
        grid_spec=grid_spec,
        out_shape=jax.ShapeDtypeStruct((b, s, 256), F32),
        compiler_params=_cparams(("parallel", "arbitrary")),
        name="moba_prompt",
    )(ti, tj, mq, m_row, m_row, kmean, bias_tiles)


def _compress_kernel(r_ref, w1_ref, pos_ref, w1raw_ref, w2_ref, gkc_ref, kc_ref, vc_ref):
    a = jnp.dot(r_ref[0], w1_ref[...], preferred_element_type=F32, precision=HI)
    n = a.shape[0]
    outs = (kc_ref, vc_ref)
    for kv in range(2):
        pos = jnp.broadcast_to(pos_ref[kv], (8, pos_ref.shape[-1]))
        cst = jnp.dot(pos, w1raw_ref[kv], preferred_element_type=F32, precision=HI)[0:1]
        top = a[:, 256 * kv:256 * kv + 128]
        bot = a[:, 256 * kv + 128:256 * kv + 256]
        hid = top + pltpu.roll(bot, n - 1, 0) + cst
        out = jnp.dot(jax.nn.gelu(hid), w2_ref[kv], preferred_element_type=F32, precision=HI)
        if kv == 0:
            out = out * lax.rsqrt(jnp.mean(out * out, axis=-1, keepdims=True) + EPS) * gkc_ref[...]
        outs[kv][0] = out


def _nsa_compress(ckv, w1cat, pos_flat, w1raw, w2x4, gkc4):
    b, s, _ = ckv.shape
    n = s // NSA_CMP_STRIDE
    r = ckv.reshape(b, n, NSA_CMP_STRIDE * 128)
    return pl.pallas_call(
        _compress_kernel,
        grid=(b,),
        in_specs=[
            pl.BlockSpec((1, n, NSA_CMP_STRIDE * 128), lambda i: (i, 0, 0)),
            pl.BlockSpec((NSA_CMP_STRIDE * 128, 512), lambda i: (0, 0)),
            pl.BlockSpec((2, 1, NSA_CMP_BLOCK * HEAD_DIM), lambda i: (0, 0, 0)),
            pl.BlockSpec((2, NSA_CMP_BLOCK * HEAD_DIM, NSA_CMP_HIDDEN), lambda i: (0, 0, 0)),
            pl.BlockSpec((2, NSA_CMP_HIDDEN, 256), lambda i: (0, 0, 0)),
            pl.BlockSpec((1, 256), lambda i: (0, 0)),
        ],
        out_specs=[pl.BlockSpec((1, n, 256), lambda i: (i, 0, 0))] * 2,
        out_shape=[jax.ShapeDtypeStruct((b, n, 256), F32)] * 2,
        compiler_params=_cparams(("parallel",)),
        name="nsa_compress",
    )(r, w1cat, pos_flat, w1raw, w2x4, gkc4)


def _nsa_cmp_kernel(q_ref, kc_ref, vc_ref, ovl_ref, o_ref, sel_ref):
    i = pl.program_id(1)
    ncmp = kc_ref.shape[1]
    nsel = ovl_ref.shape[1]
    q = q_ref[0]
    head = lax.broadcasted_iota(jnp.int32, (1, 256), 1) // HEAD_DIM
    tpos = i * TQ + lax.broadcasted_iota(jnp.int32, (TQ, 1), 0)
    blk_end = lax.broadcasted_iota(jnp.int32, (1, ncmp), 1) * NSA_CMP_STRIDE + (NSA_CMP_BLOCK - 1)
    valid = blk_end <= tpos
    vc = vc_ref[0].astype(BF16)
    imp = jnp.zeros((TQ, nsel), F32)
    o = jnp.zeros((TQ, 256), F32)
    for h in range(NSA_HEADS):
        qh = jnp.where(head == h, q, 0.0)
        s = jnp.where(valid, _dot_nt(qh, kc_ref[0], precision=HI) * SCALE, NEG)
        p = jnp.where(valid, jnp.exp(s - jnp.max(s, axis=-1, keepdims=True)), 0.0)
        p = p / jnp.maximum(jnp.sum(p, axis=-1, keepdims=True), 1e-30)
        o = jnp.where(head == h, jnp.dot(p.astype(BF16), vc, preferred_element_type=F32), o)
        imp = imp + jnp.dot(p, ovl_ref[...], preferred_element_type=F32, precision=HI)
    o_ref[0] = o
    jcol = lax.broadcasted_iota(jnp.int32, (1, nsel), 1)
    cur = tpos // NSA_SEL_BLOCK
    forced = (jcol == 0) | (jcol == cur) | (jcol == cur - 1)
    score = jnp.where(jcol <= cur, imp + jnp.where(forced, FORCE_BONUS, 0.0), NEG)
    sel = (jcol <= cur) & (_rank_desc(score, nsel) < NSA_SEL_TOPK)
    sel_ref[0] = jnp.where(sel, 1.0, 0.0)


def _nsa_cmp_select(nq, kc4, vc4, overlap):
    b, s, _ = nq.shape
    ncmp = kc4.shape[1]
    nsel = overlap.shape[1]
    return pl.pallas_call(
        _nsa_cmp_kernel,
        grid=(b, s // TQ),
        in_specs=[
            pl.BlockSpec((1, TQ, 256), lambda b_, i: (b_, i, 0)),
            pl.BlockSpec((1, ncmp, 256), lambda b_, i: (b_, 0, 0)),
            pl.BlockSpec((1, ncmp, 256), lambda b_, i: (b_, 0, 0)),
            pl.BlockSpec((ncmp, nsel), lambda b_, i: (0, 0)),
        ],
        out_specs=[pl.BlockSpec((1, TQ, 256), lambda b_, i: (b_, i, 0)),
                   pl.BlockSpec((1, TQ, nsel), lambda b_, i: (b_, i, 0))],
        out_shape=[jax.ShapeDtypeStruct((b, s, 256), F32), jax.ShapeDtypeStruct((b, s, nsel), F32)],
        compiler_params=_cparams(("parallel", "parallel")),
        name="nsa_cmp_select",
    )(nq, kc4, vc4, overlap)


def _load_q128(q_ref, qs_ref):
    q = q_ref[0] * SCALE
    low = lax.broadcasted_iota(jnp.int32, (1, 128), 1) < HEAD_DIM
    for h in range(NSA_HEADS):
        slab = q[:, 128 * (h // 2):128 * (h // 2) + 128]
        if h % 2:
            slab = pltpu.roll(slab, HEAD_DIM, 1)
        qs_ref[TQ * h:TQ * (h + 1)] = jnp.where(low, slab, 0.0).astype(BF16)


def _kv128_step(mask, kv, tsel, bias_ref, qs_ref, m_ref, l_ref, acc_ref):
    dots = _dot_nt(qs_ref[...], kv)
    s = jnp.concatenate([jnp.where(mask, _rows(dots, h) + bias_ref[h, tsel], NEG) for h in range(NSA_HEADS)], axis=0)
    alpha, p = _softmax_update(s, m_ref, l_ref)
    acc_ref[...] = alpha * acc_ref[...] + jnp.dot(p, kv, preferred_element_type=F32)


def _store_o128(acc_ref, l_ref, o_ref):
    low = lax.broadcasted_iota(jnp.int32, (1, 128), 1) < HEAD_DIM
    om = acc_ref[...] / l_ref[...]
    for pair in range(NSA_HEADS // 2):
        even = pltpu.roll(_rows(om, 2 * pair), HEAD_DIM, 1)
        o_ref[0, :, 128 * pair:128 * pair + 128] = jnp.where(low, even, _rows(om, 2 * pair + 1))


def _nsa_sel_kernel(ti_ref, tj_ref, q_ref, kv_ref, selm_ref, exp_ref, bias_ref, o_ref,
                    qs_ref, m_ref, l_ref, acc_ref):
    t = pl.program_id(1)
    i = ti_ref[t]
    j = tj_ref[t]

    @pl.when(j == i)
    def _():
        _load_q128(q_ref, qs_ref)
        _init_state(m_ref, l_ref, acc_ref)

    kv = kv_ref[0].astype(BF16)
    picked = jnp.dot(selm_ref[0].astype(BF16), exp_ref[...], preferred_element_type=F32) > 0.5
    _kv128_step(picked, kv, jnp.minimum(i - j, 2), bias_ref, qs_ref, m_ref, l_ref, acc_ref)

    @pl.when(j == 0)
    def _():
        _store_o128(acc_ref, l_ref, o_ref)


def _nsa_win_kernel(ti_ref, tj_ref, q_ref, kv_ref, bias_ref, o_ref, qs_ref, m_ref, l_ref, acc_ref):
    t = pl.program_id(1)
    i = ti_ref[t]
    j = tj_ref[t]

    @pl.when(j == i)
    def _():
        _load_q128(q_ref, qs_ref)
        _init_state(m_ref, l_ref, acc_ref)

    kv = kv_ref[0].astype(BF16)
    dist = ((i - j) * TQ + lax.broadcasted_iota(jnp.int32, (TQ, TQ), 0)
            - lax.broadcasted_iota(jnp.int32, (TQ, TQ), 1))
    _kv128_step(dist < NSA_WINDOW, kv, jnp.minimum(i - j, 2), bias_ref, qs_ref, m_ref, l_ref, acc_ref)

    @pl.when((j == 0) | (i - j == NSA_WINDOW // TQ))
    def _():
        _store_o128(acc_ref, l_ref, o_ref)


def _kv128_scratch():
    return [
        pltpu.VMEM((4 * TQ, 128), BF16),
        pltpu.VMEM((4 * TQ, 1), F32),
        pltpu.VMEM((4 * TQ, 1), F32),
        pltpu.VMEM((4 * TQ, 128), F32),
    ]


def _nsa_selected(nq, n_row, selmask, expand, bias_tiles):
    b, s, _ = nq.shape
    nsel = selmask.shape[-1]
    ti, tj = _tri_tables(s // TQ)
    grid_spec = pltpu.PrefetchScalarGridSpec(
        num_scalar_prefetch=2,
        grid=(b, int(ti.shape[0])),
        in_specs=[
            pl.BlockSpec((1, TQ, 256), lambda b_, t, ti_, tj_: (b_, ti_[t], 0)),
            pl.BlockSpec((1, TQ, 128), lambda b_, t, ti_, tj_: (b_, tj_[t], 1)),
            pl.BlockSpec((1, TQ, nsel), lambda b_, t, ti_, tj_: (b_, ti_[t], 0)),
            pl.BlockSpec((nsel, TQ), lambda b_, t, ti_, tj_: (0, tj_[t])),
            pl.BlockSpec((4, 3, TQ, TQ), lambda b_, t, ti_, tj_: (2, 0, 0, 0)),
        ],
        out_specs=pl.BlockSpec((1, TQ, 256), lambda b_, t, ti_, tj_: (b_, ti_[t], 0)),
        scratch_shapes=_kv128_scratch(),
    )
    return pl.pallas_call(
        _nsa_sel_kernel,
        grid_spec=grid_spec,
        out_shape=jax.ShapeDtypeStruct((b, s, 256), F32),
        compiler_params=_cparams(("parallel", "arbitrary")),
        name="nsa_selected",
    )(ti, tj, nq, n_row, selmask, expand, bias_tiles)


def _nsa_window(nq, w_row, bias_tiles):
    b, s, _ = nq.shape
    ti, tj = _tri_tables(s // TQ, depth=NSA_WINDOW // TQ + 1)
    grid_spec = pltpu.PrefetchScalarGridSpec(
        num_scalar_prefetch=2,
        grid=(b, int(ti.shape[0])),
        in_specs=[
            pl.BlockSpec((1, TQ, 256), lambda b_, t, ti_, tj_: (b_, ti_[t], 0)),
            pl.BlockSpec((1, TQ, 128), lambda b_, t, ti_, tj_: (b_, tj_[t], 0)),
            pl.BlockSpec((4, 3, TQ, TQ), lambda b_, t, ti_, tj_: (2, 0, 0, 0)),
        ],
        out_specs=pl.BlockSpec((1, TQ, 256), lambda b_, t, ti_, tj_: (b_, ti_[t], 0)),
        scratch_shapes=_kv128_scratch(),
    )
    return pl.pallas_call(
        _nsa_win_kernel,
        grid_spec=grid_spec,
        out_shape=jax.ShapeDtypeStruct((b, s, 256), F32),
        compiler_params=_cparams(("parallel", "arbitrary")),
        name="nsa_window",
    )(ti, tj, nq, w_row, bias_tiles)


def _merge_kernel(x_ref, g_ref, oa_ref, ob_ref, ocmp_ref, osel_ref, owin_ref, ngate_ref,
                  wbg_ref, wa_ref, wb_ref, wc_ref, wout_ref, gexp_ref, o_ref):
    x = x_ref[...]
    u = _rms(x, g_ref[...]).astype(BF16)
    ng = jax.nn.sigmoid(ngate_ref[...])
    gx = jnp.dot(ng, gexp_ref[...], preferred_element_type=F32, precision=HI)
    oc = gx[:, 0:256] * ocmp_ref[...] + gx[:, 256:512] * osel_ref[...] + gx[:, 512:768] * owin_ref[...]
    branches = ((oa_ref[...], wa_ref), (ob_ref[...], wb_ref), (oc, wc_ref))
    hsum = jnp.zeros(x.shape, F32)
    for k, (ob, w_ref) in enumerate(branches):
        gate = jax.nn.sigmoid(jnp.dot(u, wbg_ref[:, D_MODEL * k:D_MODEL * (k + 1)], preferred_element_type=F32))
        hsum = hsum + gate * jnp.dot(ob.astype(BF16), w_ref[...], preferred_element_type=F32)
    o_ref[...] = x + jnp.dot(hsum.astype(BF16), wout_ref[...], preferred_element_type=F32)


def _merge(x, g, o_a, o_b, o_cmp, o_sel, o_win, ngate, wbg, wa, wb, wc, wout, gexp, tm):
    n = x.shape[0]
    row = lambda w: pl.BlockSpec((tm, w), lambda i: (i, 0))
    full = lambda a: pl.BlockSpec(a.shape, lambda i: (0,) * a.ndim)
    return pl.pallas_call(
        _merge_kernel,
        grid=(n // tm,),
        in_specs=[row(D_MODEL), full(g), row(256), row(512), row(256), row(256), row(256), row(128),
                  full(wbg), full(wa), full(wb), full(wc), full(wout), full(gexp)],
        out_specs=row(D_MODEL),
        out_shape=jax.ShapeDtypeStruct((n, D_MODEL), F32),
        compiler_params=_cparams(("parallel",)),
        name="merge",
    )(x, g, o_a, o_b, o_cmp, o_sel, o_win, ngate, wbg, wa, wb, wc, wout, gexp)


def _rel_bucket(dist):
    n = jnp.maximum(dist, 0)
    exact = N_BUCKETS // 2
    nf = jnp.maximum(n, 1).astype(F32)
    large = exact + (jnp.log(nf / exact) / math.log(MAX_DISTANCE / exact) * (N_BUCKETS - exact)).astype(jnp.int32)
    return jnp.where(n < exact, n, jnp.minimum(large, N_BUCKETS - 1))


def _head_bias(table, dist):
    hidx = jnp.arange(table.shape[1]).reshape(-1, 1, 1)
    return table.T[hidx, _rel_bucket(dist)].astype(F32)


def _rms_norm(x, g):
    xf = x.astype(F32)
    y = xf * lax.rsqrt(jnp.mean(xf * xf, axis=-1, keepdims=True) + EPS)
    return (y * g.astype(F32)).astype(x.dtype)


def _masked_softmax(logits, mask):
    logits = jnp.where(mask, logits, NEG)
    m = jnp.max(logits, axis=-1, keepdims=True)
    p = jnp.where(mask, jnp.exp(logits - m), 0.0)
    return p / jnp.maximum(jnp.sum(p, axis=-1, keepdims=True), 1e-30)


def _gather_rows(pool_l, page_table, new_rows, pos, head=None):
    b = jnp.arange(pos.shape[0]).reshape((-1,) + (1,) * (pos.ndim - 1))
    pc = jnp.clip(pos, 0, PAST_LEN - 1)
    phys = page_table[b, pc // PAGE_SIZE]
    pn = jnp.clip(pos - PAST_LEN, 0, new_rows.shape[1] - 1)
    if head is None:
        past = pool_l[phys, pc % PAGE_SIZE]
        new = new_rows[b, pn]
    else:
        past = pool_l[phys, pc % PAGE_SIZE, :, head]
        new = new_rows[b, pn, :, head]
    is_past = (pos < PAST_LEN).reshape(pos.shape + (1,) * (past.ndim - pos.ndim))
    return jnp.where(is_past, past, new)


def _block_means(k, nb):
    bsz, length, h, d = k.shape
    kb = jnp.pad(k, ((0, 0), (0, nb * MOBA_BLOCK - length), (0, 0), (0, 0))).reshape(bsz, nb, MOBA_BLOCK, h, d)
    return jnp.mean(kb.astype(F32), axis=2), kb


def _moba_select(q, kmean, own):
    nb = kmean.shape[1]
    gate = jnp.einsum('bqhd,bnhd->bhqn', q, kmean, preferred_element_type=F32)
    gate = jnp.where(jnp.arange(nb)[None, :] < own[:, None], gate, NEG)
    _, sel = lax.top_k(gate, min(MOBA_TOPK, nb))
    return sel


def _moba_decode(q, m_new, pool_l, page_table, table):
    db, qn, h, d = q.shape
    length = PAST_LEN + qn
    t = PAST_LEN + jnp.arange(qn)
    k_past = pool_l[page_table, :, 0].reshape(db, PAST_LEN, h, d)
    kmean, _ = _block_means(jnp.concatenate([k_past, m_new[:, :, 0]], axis=1), -(-length // MOBA_BLOCK))
    own = t // MOBA_BLOCK
    sel = _moba_select(q, kmean, own)
    n = sel.shape[-1]
    ar = jnp.arange(MOBA_BLOCK)
    pos_sel = (sel[..., None] * MOBA_BLOCK + ar).reshape(db, h, qn, n * MOBA_BLOCK)
    ok_sel = jnp.broadcast_to((sel < own[:, None])[..., None], sel.shape + (MOBA_BLOCK,)).reshape(pos_sel.shape)
    pos_own = own[:, None] * MOBA_BLOCK + ar
    pos = jnp.concatenate([pos_sel, jnp.broadcast_to(pos_own, (db, h, qn, MOBA_BLOCK))], -1)
    mask = jnp.concatenate([ok_sel, jnp.broadcast_to(pos_own <= t[:, None], (db, h, qn, MOBA_BLOCK))], -1)
    kv = _gather_rows(pool_l, page_table, m_new, pos, jnp.arange(h).reshape(1, h, 1, 1))
    logits = jnp.einsum('bqhd,bhqkd->bhqk', q, kv[..., 0, :], preferred_element_type=F32) * SCALE
    p = _masked_softmax(logits + _head_bias(table, t[:, None] - pos), mask)
    o = jnp.einsum('bhqk,bhqkd->bqhd', p, kv[..., 1, :])
    return o.reshape(db, qn, h * d)


def _diff_combine(om, lam, lam_init, subln):
    lamf = lam.astype(F32)
    lam_full = jnp.exp(jnp.sum(lamf[0] * lamf[1])) - jnp.exp(jnp.sum(lamf[2] * lamf[3])) + lam_init
    o = om[..., 0, :] - lam_full * om[..., 1, :]
    o = _rms_norm(o, subln) * (1.0 - lam_init)
    return o.reshape(o.shape[:-2] + (-1,))


def _diff_decode(q, d_row, pool_l, page_table, table, lam, lam_init, subln):
    db, qn, h, _, d = q.shape
    t = PAST_LEN + jnp.arange(qn)

    def update(carry, k, v, pos):
        m, s, acc = carry
        logits = jnp.einsum('bqhmd,bkhmd->bhmqk', q, k, preferred_element_type=F32) * SCALE
        logits = logits + _head_bias(table, t[:, None] - pos[None, :])[:, None]
        mask = pos[None, :] <= t[:, None]
        logits = jnp.where(mask, logits, NEG)
        m_new = jnp.maximum(m, jnp.max(logits, axis=-1))
        p = jnp.where(mask, jnp.exp(logits - m_new[..., None]), 0.0)
        corr = jnp.exp(m - m_new)
        s = s * corr + jnp.sum(p, axis=-1)
        acc = acc * corr[..., None] + jnp.einsum('bhmqk,bkhe->bhmqe', p, v, preferred_element_type=F32)
        return (m_new, s, acc)

    def page_step(carry, xs_):
        j, phys = xs_
        kv = pool_l[phys]
        k = kv[:, :, 0].reshape(db, PAGE_SIZE, h, 2, d)
        return update(carry, k, kv[:, :, 1], j * PAGE_SIZE + jnp.arange(PAGE_SIZE)), None

    init = (jnp.full((db, h, 2, qn), NEG, F32), jnp.zeros((db, h, 2, qn), F32),
            jnp.zeros((db, h, 2, qn, DIFF_VDIM), F32))
    carry, _ = lax.scan(page_step, init, (jnp.arange(page_table.shape[1]), page_table.T))
    _, s, acc = update(carry, d_row[:, :, 0].reshape(db, qn, h, 2, d), d_row[:, :, 1], t)
    om = jnp.moveaxis(acc / s[..., None], 3, 1)
    return _diff_combine(om, lam, lam_init, subln)


def _compress(rows, pos_emb, w1, w2):
    bsz, length, d = rows.shape
    n_cmp = (length - NSA_CMP_BLOCK) // NSA_CMP_STRIDE + 1
    idx = jnp.arange(n_cmp)[:, None] * NSA_CMP_STRIDE + jnp.arange(NSA_CMP_BLOCK)[None, :]
    blk = (rows[:, idx] + pos_emb).reshape(bsz, n_cmp, NSA_CMP_BLOCK * d)
    return jax.nn.gelu(blk @ w1) @ w2


def _nsa_compressed(q, ck, cv, t, g_kc, cmp_pos, cmp_w1, cmp_w2):
    kc = _rms_norm(_compress(ck, cmp_pos[0], cmp_w1[0], cmp_w2[0]), g_kc)
    vc = _compress(cv, cmp_pos[1], cmp_w1[1], cmp_w2[1])
    starts = jnp.arange(kc.shape[1]) * NSA_CMP_STRIDE
    logits = jnp.einsum('bqhd,bnd->bhqn', q, kc, preferred_element_type=F32) * SCALE
    p = _masked_softmax(logits, (starts + NSA_CMP_BLOCK - 1)[None, :] <= t[:, None])
    o = jnp.einsum('bhqn,bnd->bqhd', p, vc)
    lo = jnp.arange(-(-ck.shape[1] // NSA_SEL_BLOCK)) * NSA_SEL_BLOCK
    overlap = ((starts[:, None] < lo[None, :] + NSA_SEL_BLOCK)
               & (starts[:, None] + NSA_CMP_BLOCK > lo[None, :])).astype(F32)
    imp = jnp.einsum('bhqn,nj->bqj', p, overlap)
    return o, imp


def _nsa_select_blocks(imp, t):
    j = jnp.arange(imp.shape[-1])[None, :]
    cur = (t // NSA_SEL_BLOCK)[:, None]
    forced = (j == 0) | (j == cur) | (j == cur - 1)
    score = jnp.where(j <= cur, imp + FORCE_BONUS * forced, NEG)
    _, idx = lax.top_k(score, min(NSA_SEL_TOPK, imp.shape[-1]))
    return idx


def _nsa_sel_attend(q, ks, vs, pos, t, table):
    dist = (t[None, :, None] - pos)[:, None]
    logits = jnp.einsum('bqhd,bqkd->bhqk', q, ks, preferred_element_type=F32) * SCALE
    p = _masked_softmax(logits + _head_bias(table, dist), dist >= 0)
    return jnp.einsum('bhqk,bqkd->bqhd', p, vs)


def _nsa_window_decode(q, buf, w_new, t, table):
    w_buf = buf.shape[1]
    kv = jnp.concatenate([buf, w_new], axis=1)
    pos = PAST_LEN - w_buf + jnp.arange(kv.shape[1])
    dist = t[:, None] - pos[None, :]
    logits = jnp.einsum('bqhd,bkd->bhqk', q, kv[:, :, 0], preferred_element_type=F32) * SCALE
    p = _masked_softmax(logits + _head_bias(table, dist), (dist >= 0) & (dist < NSA_WINDOW))
    o = jnp.einsum('bhqk,bkd->bqhd', p, kv[:, :, 1])
    return o, kv[:, kv.shape[1] - w_buf:]


def _nsa_decode(q, n_new, w_new, pool_l, page_table, buf, table, g_kc, cmp_pos, cmp_w1, cmp_w2):
    db, qn, h, d = q.shape
    t = PAST_LEN + jnp.arange(qn)
    past_c = pool_l[page_table, :, :2].reshape(db, PAST_LEN, 2, d)
    all_c = jnp.concatenate([past_c, n_new[:, :, :2]], axis=1)
    o_cmp, imp = _nsa_compressed(q, all_c[:, :, 0], all_c[:, :, 1], t, g_kc, cmp_pos, cmp_w1, cmp_w2)
    idx = _nsa_select_blocks(imp, t)
    pos = (idx[..., None] * NSA_SEL_BLOCK + jnp.arange(NSA_SEL_BLOCK)).reshape(db, qn, -1)
    rows = _gather_rows(pool_l, page_table, n_new, pos)
    o_sel = _nsa_sel_attend(q, rows[..., 2, :], rows[..., 3, :], pos, t, table)
    o_win, new_buf = _nsa_window_decode(q, buf, w_new, t, table)
    return o_cmp, o_sel, o_win, new_buf


PAGES_PER_STEP = 16
N_PAGES = PAST_LEN // PAGE_SIZE
OWN_BLOCK = PAST_LEN // MOBA_BLOCK
CUR_SEL = PAST_LEN // NSA_SEL_BLOCK
IMP_W = 384


def _page_of(pt, b, slot, n_pool):
    b = jnp.clip(b, 0, pt.shape[0] - 1)
    return jnp.clip(pt[b, jnp.clip(slot, 0, N_PAGES - 1)], 0, n_pool - 1)


def _picked(sl, b, k):
    return sl[jnp.clip(b, 0, sl.shape[0] - 1), k]


def _row_of_heads(x, n):
    return jnp.concatenate([x[h:h + 1] for h in range(n)], axis=1)


def _diff_dec_kernel(pt_ref, q_ref, knew_ref, vnew_ref, bias_ref, b0_ref, lam_ref, subln_ref, *rest, lam_init):
    pages = rest[:PAGES_PER_STEP]
    o_ref, m_ref, l_ref, acc_ref = rest[PAGES_PER_STEP:]
    s = pl.program_id(1)

    @pl.when(s == 0)
    def _():
        for h in range(DIFF_HEADS):
            m_ref[8 * h:8 * (h + 1)] = (jnp.sum(q_ref[0, h] * knew_ref[0, h], axis=-1, keepdims=True)
                                        + b0_ref[h][:, 0:1])
            acc_ref[8 * h:8 * (h + 1)] = jnp.broadcast_to(vnew_ref[0, h], (8, DIFF_VDIM))
        l_ref[...] = jnp.ones(l_ref.shape, F32)

    logits = []
    for h in range(DIFF_HEADS):
        qh = q_ref[0, h].astype(BF16)
        logits.append(jnp.concatenate(
            [_dot_nt(qh, pages[r][pl.ds(h, PAGE_SIZE, stride=8), :].astype(BF16)) + bias_ref[h, r:r + 1, :]
             for r in range(PAGES_PER_STEP)], axis=1))
    alpha, p = _softmax_update(jnp.concatenate(logits, axis=0), m_ref, l_ref, narrow=False)
    pvs = []
    for h in range(DIFF_HEADS):
        pv = jnp.zeros((8, DIFF_VDIM), F32)
        for r in range(PAGES_PER_STEP):
            vh = pages[r][pl.ds(DIFF_HEADS + h, PAGE_SIZE, stride=8), :].astype(BF16)
            pv = pv + jnp.dot(p[8 * h:8 * (h + 1), PAGE_SIZE * r:PAGE_SIZE * (r + 1)].astype(BF16), vh,
                              preferred_element_type=F32)
        pvs.append(pv)
    acc_ref[...] = alpha * acc_ref[...] + jnp.concatenate(pvs, axis=0)

    @pl.when(s == pl.num_programs(1) - 1)
    def _():
        lam = lam_ref[...]
        lam_full = (jnp.exp(jnp.sum(lam[0:1] * lam[1:2], axis=-1, keepdims=True))
                    - jnp.exp(jnp.sum(lam[2:3] * lam[3:4], axis=-1, keepdims=True)) + lam_init)
        om = acc_ref[...] / l_ref[...]
        for h in range(DIFF_HEADS):
            d = om[8 * h:8 * h + 1] - lam_full * om[8 * h + 1:8 * h + 2]
            o_ref[0, :, DIFF_VDIM * h:DIFF_VDIM * (h + 1)] = _rms(d, subln_ref[...]) * (1.0 - lam_init)


def _diff_dec(l, dq, d_new, pages, page_table, bias_past, b0, lam, subln, lam_init):
    db = dq.shape[0]
    q4 = (dq * SCALE).reshape(db, DIFF_HEADS, 2, HEAD_DIM)
    q = jnp.zeros((db, DIFF_HEADS, 8, 128), F32)
    q = q.at[:, :, 0, :HEAD_DIM].set(q4[:, :, 0]).at[:, :, 1, HEAD_DIM:].set(q4[:, :, 1])
    knew = d_new[:, :512].reshape(db, DIFF_HEADS, 1, 128)
    vnew = d_new[:, 512:].reshape(db, DIFF_HEADS, 1, 128)
    page_specs = [
        pl.BlockSpec((None, None, PAGE_SIZE * 8, 128),
                     lambda b, s, pt, r=r: (l, _page_of(pt, b, PAGES_PER_STEP * s + r, pages.shape[1]), 0, 0))
        for r in range(PAGES_PER_STEP)]
    grid_spec = pltpu.PrefetchScalarGridSpec(
        num_scalar_prefetch=1,
        grid=(db, N_PAGES // PAGES_PER_STEP),
        in_specs=[
            pl.BlockSpec((1, DIFF_HEADS, 8, 128), lambda b, s, pt: (b, 0, 0, 0)),
            pl.BlockSpec((1, DIFF_HEADS, 1, 128), lambda b, s, pt: (b, 0, 0, 0)),
            pl.BlockSpec((1, DIFF_HEADS, 1, 128), lambda b, s, pt: (b, 0, 0, 0)),
            pl.BlockSpec((4, PAGES_PER_STEP, PAGE_SIZE), lambda b, s, pt: (1, s, 0)),
            pl.BlockSpec((4, 1, 128), lambda b, s, pt: (1, 0, 0)),
            pl.BlockSpec((4, HEAD_DIM), lambda b, s, pt: (0, 0)),
            pl.BlockSpec((1, DIFF_VDIM), lambda b, s, pt: (0, 0)),
        ] + page_specs,
        out_specs=pl.BlockSpec((1, 1, 512), lambda b, s, pt: (b, 0, 0)),
        scratch_shapes=[pltpu.VMEM((8 * DIFF_HEADS, 1), F32), pltpu.VMEM((8 * DIFF_HEADS, 1), F32),
                        pltpu.VMEM((8 * DIFF_HEADS, DIFF_VDIM), F32)],
    )
    out = pl.pallas_call(
        functools.partial(_diff_dec_kernel, lam_init=lam_init),
        grid_spec=grid_spec,
        out_shape=jax.ShapeDtypeStruct((db, 1, 512), F32),
        compiler_params=_cparams(("parallel", "arbitrary")),
        name="diff_decode",
    )(page_table, q, knew, vnew, bias_past, b0, lam, subln, *([pages] * PAGES_PER_STEP))
    return out.reshape(db, 512)


def _moba_gate_kernel(pt_ref, qcol_ref, *rest):
    pages = rest[:PAGES_PER_STEP]
    sel_ref, gate_ref = rest[PAGES_PER_STEP:]
    s = pl.program_id(1)
    lane = lax.broadcasted_iota(jnp.int32, (1, 1, 128), 2)

    @pl.when(s == 0)
    def _():
        gate_ref[...] = jnp.zeros(gate_ref.shape, F32)

    qb = qcol_ref[0]
    per_block = MOBA_BLOCK // PAGE_SIZE
    for blk in range(PAGES_PER_STEP // per_block):
        part = jnp.zeros((MOBA_HEADS, 8, PAGE_SIZE), F32)
        for half in range(per_block):
            prod = pages[per_block * blk + half][...] * qb
            part = part + jnp.sum(prod.reshape(MOBA_HEADS, HEAD_DIM // 8, 8, PAGE_SIZE), axis=1)
        tot = jnp.sum(jnp.sum(part, axis=2, keepdims=True), axis=1, keepdims=True) * (1.0 / MOBA_BLOCK)
        j = s * (PAGES_PER_STEP // per_block) + blk
        gate_ref[...] = jnp.where(lane == j, tot, gate_ref[...])

    @pl.when(s == pl.num_programs(1) - 1)
    def _():
        score = jnp.where(lane < OWN_BLOCK, gate_ref[...], NEG)
        rank = _rank_desc(score, OWN_BLOCK)
        lane_f = lane.astype(F32)
        out = jnp.zeros(score.shape, jnp.int32)
        for r in range(MOBA_TOPK):
            idx = jnp.sum(jnp.where(rank == r, lane_f, 0.0), axis=2, keepdims=True).astype(jnp.int32)
            out = jnp.where(lane == r, idx, out)
        sel_ref[0] = out


def _moba_gate(l, mq, pages, page_table):
    db = mq.shape[0]
    qcol = jnp.broadcast_to(mq.reshape(db, MOBA_HEADS, HEAD_DIM, 1), (db, MOBA_HEADS, HEAD_DIM, PAGE_SIZE))
    page_specs = [
        pl.BlockSpec((None, None, None, MOBA_HEADS, HEAD_DIM, PAGE_SIZE),
                     lambda b, s, pt, r=r: (l, _page_of(pt, b, PAGES_PER_STEP * s + r, pages.shape[1]), 0, 0, 0, 0))
        for r in range(PAGES_PER_STEP)]
    grid_spec = pltpu.PrefetchScalarGridSpec(
        num_scalar_prefetch=1,
        grid=(db, N_PAGES // PAGES_PER_STEP),
        in_specs=[pl.BlockSpec((1, MOBA_HEADS, HEAD_DIM, PAGE_SIZE), lambda b, s, pt: (b, 0, 0, 0))] + page_specs,
        out_specs=pl.BlockSpec((1, MOBA_HEADS, 1, 128), lambda b, s, pt: (b, 0, 0, 0)),
        scratch_shapes=[pltpu.VMEM((MOBA_HEADS, 1, 128), F32)],
    )
    sel = pl.pallas_call(
        _moba_gate_kernel,
        grid_spec=grid_spec,
        out_shape=jax.ShapeDtypeStruct((db, MOBA_HEADS, 1, 128), jnp.int32),
        compiler_params=_cparams(("parallel", "arbitrary")),
        name="moba_gate",
    )(page_table, qcol, *([pages] * PAGES_PER_STEP))
    sel = jnp.clip(sel[:, :, 0, :MOBA_TOPK].reshape(db, MOBA_HEADS * MOBA_TOPK), 0, OWN_BLOCK - 1)
    return jnp.pad(sel, ((0, 0), (0, 128 - MOBA_HEADS * MOBA_TOPK)))


def _moba_dec_kernel(pt_ref, sel_ref, q_ref, knew_ref, vnew_ref, bias_ref, b0_ref, *rest):
    per_head = MOBA_TOPK * (MOBA_BLOCK // PAGE_SIZE)
    pages = rest[:MOBA_HEADS * per_head]
    o_ref = rest[MOBA_HEADS * per_head]
    b = pl.program_id(0)
    outs = []
    for h in range(MOBA_HEADS):
        qh = q_ref[0, h]
        qb = qh.astype(BF16)
        s_self = jnp.sum(qh * knew_ref[0, h], axis=-1, keepdims=True) + b0_ref[h][:, 0:1]
        logits = []
        for i in range(per_head):
            page = jnp.clip((MOBA_BLOCK // PAGE_SIZE) * sel_ref[b, MOBA_TOPK * h + i // 2] + i % 2, 0, N_PAGES - 1)
            kt = pages[per_head * h + i][0].astype(BF16)
            logits.append(jnp.dot(qb, kt, preferred_element_type=F32) + bias_ref[h, pl.ds(page, 1), :])
        sc = jnp.concatenate(logits, axis=1)
        m = jnp.maximum(jnp.max(sc, axis=-1, keepdims=True), s_self)
        p = jnp.exp(sc - m)
        p_self = jnp.exp(s_self - m)
        den = jnp.sum(p, axis=-1, keepdims=True) + p_self
        pv = p_self * vnew_ref[0, h]
        for i in range(per_head):
            vt = pages[per_head * h + i][1].astype(BF16)
            pv = pv + _dot_nt(p[:, PAGE_SIZE * i:PAGE_SIZE * (i + 1)].astype(BF16), vt)
        outs.append((pv / den)[0:1])
    o_ref[0] = jnp.concatenate(outs, axis=1)


def _moba_dec(l, mq, m_new, sel, pages, page_table, bias_past, b0):
    db = mq.shape[0]
    q = jnp.zeros((db, MOBA_HEADS, 8, HEAD_DIM), F32).at[:, :, 0].set((mq * SCALE).reshape(db, MOBA_HEADS, HEAD_DIM))
    knew = m_new[:, :256].reshape(db, MOBA_HEADS, 1, HEAD_DIM)
    vnew = m_new[:, 256:].reshape(db, MOBA_HEADS, 1, HEAD_DIM)
    per_block = MOBA_BLOCK // PAGE_SIZE
    n_pool = pages.shape[1]
    page_specs = [
        pl.BlockSpec((None, None, 2, None, HEAD_DIM, PAGE_SIZE),
                     lambda b, pt, sl, h=h, r=r, half=half:
                     (l, _page_of(pt, b, per_block * _picked(sl, b, MOBA_TOPK * h + r) + half, n_pool), 0, h, 0, 0))
        for h in range(MOBA_HEADS) for r in range(MOBA_TOPK) for half in range(per_block)]
    grid_spec = pltpu.PrefetchScalarGridSpec(
        num_scalar_prefetch=2,
        grid=(db,),
        in_specs=[
            pl.BlockSpec((1, MOBA_HEADS, 8, HEAD_DIM), lambda b, pt, sl: (b, 0, 0, 0)),
            pl.BlockSpec((1, MOBA_HEADS, 1, HEAD_DIM), lambda b, pt, sl: (b, 0, 0, 0)),
            pl.BlockSpec((1, MOBA_HEADS, 1, HEAD_DIM), lambda b, pt, sl: (b, 0, 0, 0)),
            pl.BlockSpec((4, N_PAGES, PAGE_SIZE), lambda b, pt, sl: (0, 0, 0)),
            pl.BlockSpec((4, 1, 128), lambda b, pt, sl: (0, 0, 0)),
        ] + page_specs,
        out_specs=pl.BlockSpec((1, 1, 256), lambda b, pt, sl: (b, 0, 0)),
    )
    out = pl.pallas_call(
        _moba_dec_kernel,
        grid_spec=grid_spec,
        out_shape=jax.ShapeDtypeStruct((db, 1, 256), F32),
        compiler_params=_cparams(("parallel",)),
        name="moba_decode",
    )(page_table, sel, q, knew, vnew, bias_past, b0, *([pages] * len(page_specs)))
    return out.reshape(db, 256)


def _nsa_cmp_dec_kernel(pt_ref, q_ref, w1_ref, pos_ref, w1raw_ref, w2_ref, gkc_ref, ovl_ref, *rest):
    pages = rest[:PAGES_PER_STEP]
    ocmp_ref, sel_ref, xt_ref, cst_ref, carry_ref, m_ref, l_ref, acc_ref, imp_ref = rest[PAGES_PER_STEP:]
    b = pl.program_id(0)
    s = pl.program_id(1)
    rows = PAGES_PER_STEP * PAGE_SIZE // NSA_CMP_STRIDE

    @pl.when((b == 0) & (s == 0))
    def _():
        for kv in range(2):
            pos = jnp.broadcast_to(pos_ref[kv], (8, pos_ref.shape[-1]))
            cst_ref[:, 128 * kv:128 * (kv + 1)] = jnp.dot(
                pos, w1raw_ref[kv], preferred_element_type=F32, precision=HI)[0:1]

    @pl.when(s == 0)
    def _():
        _init_state(m_ref, l_ref, acc_ref)
        imp_ref[...] = jnp.zeros(imp_ref.shape, F32)
        carry_ref[...] = jnp.zeros(carry_ref.shape, F32)

    for r in range(PAGES_PER_STEP):
        xt_ref[PAGE_SIZE * r:PAGE_SIZE * (r + 1), :] = pages[r][...].reshape(2 * HEAD_DIM, PAGE_SIZE).T
    a = jnp.zeros((rows, 512), F32)
    for rr in range(NSA_CMP_STRIDE):
        lhs = xt_ref[pl.ds(rr, rows, stride=NSA_CMP_STRIDE), :].astype(BF16)
        a = a + jnp.dot(lhs, w1_ref[rr], preferred_element_type=F32)
    first = lax.broadcasted_iota(jnp.int32, (rows, 1), 0) == 0
    outs = []
    for kv in range(2):
        top = a[:, 256 * kv:256 * kv + 128]
        bot = a[:, 256 * kv + 128:256 * kv + 256]
        top_prev = jnp.where(first, carry_ref[:, 128 * kv:128 * (kv + 1)], pltpu.roll(top, 1, 0))
        hid = top_prev + bot + cst_ref[:, 128 * kv:128 * (kv + 1)]
        outs.append(jnp.dot(jax.nn.gelu(hid), w2_ref[kv], preferred_element_type=F32, precision=HI))
        carry_ref[:, 128 * kv:128 * (kv + 1)] = top[rows - 1:rows]
    kc = _rms(outs[0], gkc_ref[...])
    vc = outs[1].astype(BF16)

    q = q_ref[0] * SCALE
    col = lax.broadcasted_iota(jnp.int32, (1, rows), 1)
    sc = jnp.where(col + s > 0, _dot_nt(q, kc, precision=HI), NEG)
    m_old = m_ref[0]
    m_new = jnp.maximum(m_old, jnp.max(sc, axis=-1, keepdims=True))
    alpha = jnp.exp(m_old - m_new)
    p = jnp.exp(sc - m_new)
    l_ref[0] = alpha * l_ref[0] + jnp.sum(p, axis=-1, keepdims=True)
    acc_ref[0] = alpha * acc_ref[0] + jnp.dot(p.astype(BF16), vc, preferred_element_type=F32)
    imp_ref[...] = alpha * imp_ref[...] + jnp.dot(p, ovl_ref[...], preferred_element_type=F32, precision=HI)
    m_ref[0] = m_new

    @pl.when(s == pl.num_programs(1) - 1)
    def _():
        ocmp_ref[0] = _row_of_heads(acc_ref[0] / l_ref[0], NSA_HEADS)
        impn = imp_ref[...] / l_ref[0]
        imp = impn[0:1] + impn[1:2] + impn[2:3] + impn[3:4]
        lane = lax.broadcasted_iota(jnp.int32, (1, IMP_W), 1)
        forced = (lane == 0) | (lane == CUR_SEL) | (lane == CUR_SEL - 1)
        score = jnp.where(lane <= CUR_SEL, imp + jnp.where(forced, FORCE_BONUS, 0.0), NEG)
        rank = _rank_desc(score, CUR_SEL + 1)
        lane_f = lane.astype(F32)
        lane_o = lax.broadcasted_iota(jnp.int32, (1, 128), 1)
        out = jnp.zeros((1, 128), jnp.int32)
        for r in range(NSA_SEL_TOPK):
            idx = jnp.sum(jnp.where(rank == r, lane_f, 0.0), axis=1, keepdims=True).astype(jnp.int32)
            out = jnp.where(lane_o == r, idx, out)
        sel_ref[0] = out


def _nsa_cmp_dec(l, nq, pages, page_table, w1dec, pos_flat, w1raw, w2, gkc, ovl_dec):
    db = nq.shape[0]
    q = jnp.zeros((db, 8, HEAD_DIM), F32).at[:, :NSA_HEADS].set(nq.reshape(db, NSA_HEADS, HEAD_DIM))
    rows = PAGES_PER_STEP * PAGE_SIZE // NSA_CMP_STRIDE
    page_specs = [
        pl.BlockSpec((None, None, 2, HEAD_DIM, PAGE_SIZE),
                     lambda b, s, pt, r=r: (l, _page_of(pt, b, PAGES_PER_STEP * s + r, pages.shape[1]), 0, 0, 0))
        for r in range(PAGES_PER_STEP)]
    full = lambda a: pl.BlockSpec(a.shape, lambda b, s, pt: (0,) * a.ndim)
    grid_spec = pltpu.PrefetchScalarGridSpec(
        num_scalar_prefetch=1,
        grid=(db, N_PAGES // PAGES_PER_STEP),
        in_specs=[pl.BlockSpec((1, 8, HEAD_DIM), lambda b, s, pt: (b, 0, 0)),
                  full(w1dec), full(pos_flat), full(w1raw), full(w2), full(gkc),
                  pl.BlockSpec((rows, IMP_W), lambda b, s, pt: (s, 0))] + page_specs,
        out_specs=[pl.BlockSpec((1, 1, 256), lambda b, s, pt: (b, 0, 0)),
                   pl.BlockSpec((1, 1, 128), lambda b, s, pt: (b, 0, 0))],
        scratch_shapes=[
            pltpu.VMEM((PAGES_PER_STEP * PAGE_SIZE, 128), F32),
            pltpu.VMEM((1, 256), F32),
            pltpu.VMEM((1, 256), F32),
            pltpu.VMEM((1, 8, 1), F32),
            pltpu.VMEM((1, 8, 1), F32),
            pltpu.VMEM((1, 8, HEAD_DIM), F32),
            pltpu.VMEM((8, IMP_W), F32),
        ],
    )
    o_cmp, sel = pl.pallas_call(
        _nsa_cmp_dec_kernel,
        grid_spec=grid_spec,
        out_shape=[jax.ShapeDtypeStruct((db, 1, 256), F32), jax.ShapeDtypeStruct((db, 1, 128), jnp.int32)],
        compiler_params=_cparams(("arbitrary", "arbitrary")),
        name="nsa_cmp_decode",
    )(page_table, q, w1dec, pos_flat, w1raw, w2, gkc, ovl_dec, *([pages] * PAGES_PER_STEP))
    return o_cmp.reshape(db, 256), jnp.clip(sel[:, 0, :], 0, CUR_SEL)


def _nsa_dec_kernel(pt_ref, sel_ref, q_ref, snew_ref, wnew_ref, wcol_ref, bias_ref, bwin_ref, b0_ref, win_ref,
                    *rest):
    pages = rest[:NSA_SEL_TOPK]
    osel_ref, owin_ref, buf_ref = rest[NSA_SEL_TOPK:]
    b = pl.program_id(0)
    q = q_ref[0] * SCALE
    qb = q.astype(BF16)
    b0 = b0_ref[:, 0:1]

    half_of = lax.broadcasted_iota(jnp.int32, (1, PAGE_SIZE), 1) // NSA_SEL_BLOCK
    logits = []
    for r in range(NSA_SEL_TOPK):
        j = sel_ref[b, r]
        page = jnp.clip(j // 2, 0, N_PAGES - 1)
        want = jnp.where(j < CUR_SEL, j % 2, 2)
        sc = jnp.dot(qb, pages[r][0].astype(BF16), preferred_element_type=F32) + bias_ref[page]
        logits.append(jnp.where(half_of == want, sc, NEG))
    sc = jnp.concatenate(logits, axis=1)
    s_self = jnp.sum(q * snew_ref[0, 0:1], axis=-1, keepdims=True) + b0
    m = jnp.maximum(jnp.max(sc, axis=-1, keepdims=True), s_self)
    p = jnp.exp(sc - m)
    p_self = jnp.exp(s_self - m)
    den = jnp.sum(p, axis=-1, keepdims=True) + p_self
    pv = p_self * snew_ref[0, 1:2]
    for r in range(NSA_SEL_TOPK):
        pv = pv + _dot_nt(p[:, PAGE_SIZE * r:PAGE_SIZE * (r + 1)].astype(BF16), pages[r][1].astype(BF16))
    osel_ref[0] = _row_of_heads(pv / den, NSA_HEADS)

    wlen = win_ref.shape[-1]
    lane = lax.broadcasted_iota(jnp.int32, (1, wlen), 1)
    sw = jnp.dot(qb, win_ref[0].astype(BF16), preferred_element_type=F32) + bwin_ref[...]
    sw = jnp.where(lane == 0, NEG, sw)
    s_self = jnp.sum(q * wnew_ref[0, 0:1], axis=-1, keepdims=True) + b0
    m = jnp.maximum(jnp.max(sw, axis=-1, keepdims=True), s_self)
    p = jnp.exp(sw - m)
    p_self = jnp.exp(s_self - m)
    den = jnp.sum(p, axis=-1, keepdims=True) + p_self
    pv = p_self * wnew_ref[0, 1:2] + _dot_nt(p.astype(BF16), win_ref[1].astype(BF16))
    owin_ref[0] = _row_of_heads(pv / den, NSA_HEADS)
    last = lax.broadcasted_iota(jnp.int32, (1, 128), 1) == 127
    for kv in range(2):
        shifted = pltpu.roll(win_ref[kv], wlen - 1, 1)
        buf_ref[0, kv, :, 0:wlen - 128] = shifted[:, 0:wlen - 128]
        buf_ref[0, kv, :, wlen - 128:wlen] = jnp.where(last, wcol_ref[0, kv], shifted[:, wlen - 128:wlen])


def _nsa_dec(l, nq, n_new, w_new, sel, pages, win, page_table, bias_nsa, bwin, b0n):
    db = nq.shape[0]
    wlen = win.shape[-1]
    q = jnp.zeros((db, 8, HEAD_DIM), F32).at[:, :NSA_HEADS].set(nq.reshape(db, NSA_HEADS, HEAD_DIM))
    snew = n_new[:, 128:].reshape(db, 2, HEAD_DIM)
    wnew = w_new.reshape(db, 2, HEAD_DIM)
    wcol = jnp.broadcast_to(w_new.reshape(db, 2, HEAD_DIM, 1), (db, 2, HEAD_DIM, 128))
    n_pool = pages.shape[1]
    page_specs = [
        pl.BlockSpec((None, None, 2, HEAD_DIM, PAGE_SIZE),
                     lambda b, pt, sl, r=r: (l, _page_of(pt, b, _picked(sl, b, r) // 2, n_pool), 1, 0, 0))
        for r in range(NSA_SEL_TOPK)]
    full = lambda a: pl.BlockSpec(a.shape, lambda b, pt, sl: (0,) * a.ndim)
    grid_spec = pltpu.PrefetchScalarGridSpec(
        num_scalar_prefetch=2,
        grid=(db,),
        in_specs=[pl.BlockSpec((1, 8, HEAD_DIM), lambda b, pt, sl: (b, 0, 0)),
                  pl.BlockSpec((1, 2, HEAD_DIM), lambda b, pt, sl: (b, 0, 0)),
                  pl.BlockSpec((1, 2, HEAD_DIM), lambda b, pt, sl: (b, 0, 0)),
                  pl.BlockSpec((1, 2, HEAD_DIM, 128), lambda b, pt, sl: (b, 0, 0, 0)),
                  full(bias_nsa), full(bwin), full(b0n),
                  pl.BlockSpec((None, None, 2, HEAD_DIM, wlen), lambda b, pt, sl: (l, b, 0, 0, 0))] + page_specs,
        out_specs=[pl.BlockSpec((1, 1, 256), lambda b, pt, sl: (b, 0, 0)),
                   pl.BlockSpec((1, 1, 256), lambda b, pt, sl: (b, 0, 0)),
                   pl.BlockSpec((1, 2, HEAD_DIM, wlen), lambda b, pt, sl: (b, 0, 0, 0))],
    )
    o_sel, o_win, buf = pl.pallas_call(
        _nsa_dec_kernel,
        grid_spec=grid_spec,
        out_shape=[jax.ShapeDtypeStruct((db, 1, 256), F32), jax.ShapeDtypeStruct((db, 1, 256), F32),
                   jax.ShapeDtypeStruct((db, 2, HEAD_DIM, wlen), F32)],
        compiler_params=_cparams(("parallel",)),
        name="nsa_decode",
    )(page_table, sel, q, snew, wnew, wcol, bias_nsa, bwin, b0n, win, *([pages] * NSA_SEL_TOPK))
    return o_sel.reshape(db, 256), o_win.reshape(db, 256), jnp.transpose(buf, (0, 3, 1, 2))


def _decode_bias(rel_bias, wlen):
    flat = rel_bias.T[:, _rel_bucket(PAST_LEN - jnp.arange(PAST_LEN))]
    past = flat.reshape(-1, N_PAGES, PAGE_SIZE)
    first_nsa = MOBA_HEADS + DIFF_HEADS
    bias_nsa = jnp.pad(jnp.transpose(past[first_nsa:], (1, 0, 2)), ((0, 0), (0, 8 - NSA_HEADS), (0, 0)))
    bwin = jnp.pad(flat[first_nsa:, PAST_LEN - wlen:], ((0, 8 - NSA_HEADS), (0, 0)))
    zero = rel_bias[_rel_bucket(jnp.zeros((), jnp.int32))]
    b0 = jnp.broadcast_to(zero[:, None, None], (zero.shape[0], 1, 128))
    b0n = jnp.pad(jnp.broadcast_to(zero[first_nsa:, None], (NSA_HEADS, 128)), ((0, 8 - NSA_HEADS), (0, 0)))
    return past, bias_nsa, bwin, b0, b0n


def _decode_overlap():
    n = np.arange(N_PAGES * PAGE_SIZE // NSA_CMP_STRIDE) - 1
    lo = np.arange(IMP_W) * NSA_SEL_BLOCK
    start = n[:, None] * NSA_CMP_STRIDE
    ovl = (start < lo[None, :] + NSA_SEL_BLOCK) & (start + NSA_CMP_BLOCK > lo[None, :]) & (n[:, None] >= 0)
    return jnp.asarray(ovl, F32)


def _pack_layer(l, norm_gain, ffn_w1, ffn_w2, w_in, qk_gain, cmp_pos, cmp_w1, cmp_w2,
                w_branch_moba, w_branch_diff, w_branch_nsa, w_out):
    w = w_in[l]
    w_pack = jnp.concatenate(
        [w[:, :N_MAIN], jnp.pad(w[:, N_MAIN:N_MAIN + 12], ((0, 0), (0, 116)))], axis=1).astype(BF16)
    g = qk_gain[l]
    ones = jnp.ones((HEAD_DIM,), F32)
    zeros = jnp.zeros((HEAD_DIM,), F32)

    def rep(v, k):
        return jnp.tile(v, k)

    gain_row = jnp.concatenate([
        rep(g[0], 4), rep(g[1], 4), rep(ones, 4), rep(g[2], 8), rep(g[3], 8), rep(ones, 8), rep(g[4], 4),
        ones, ones, g[6], ones, g[7], ones])[None, :]
    nmask_row = jnp.concatenate([
        rep(ones, 8), rep(zeros, 4), rep(ones, 16), rep(zeros, 8), rep(ones, 4),
        zeros, zeros, ones, zeros, ones, zeros])[None, :]
    w1 = cmp_w1[l].reshape(2, 2, NSA_CMP_STRIDE, HEAD_DIM, NSA_CMP_HIDDEN)
    zero = jnp.zeros((NSA_CMP_STRIDE, HEAD_DIM, NSA_CMP_HIDDEN), F32)
    k_cols = jnp.concatenate([w1[0, 0], w1[0, 1], zero, zero], axis=-1)
    v_cols = jnp.concatenate([zero, zero, w1[1, 0], w1[1, 1]], axis=-1)
    w1cat = jnp.concatenate([k_cols, v_cols], axis=1).reshape(NSA_CMP_STRIDE * 128, 512)
    return dict(
        g0=norm_gain[l, 0][None, :], g1=norm_gain[l, 1][None, :], g2=norm_gain[l, 2][None, :],
        ffn1=(ffn_w1[l, 0].astype(BF16), ffn_w2[l, 0].astype(BF16)),
        ffn2=(ffn_w1[l, 1].astype(BF16), ffn_w2[l, 1].astype(BF16)),
        w_pack=w_pack, gain_row=gain_row, nmask_row=nmask_row,
        wbg=w[:, N_MAIN + 12:].astype(BF16),
        w1cat=w1cat, pos_flat=cmp_pos[l].reshape(2, 1, NSA_CMP_BLOCK * HEAD_DIM), w1raw=cmp_w1[l],
        w2x4=jnp.tile(cmp_w2[l], (1, 1, 4)), gkc4=jnp.tile(g[5], 4)[None, :],
        wa=w_branch_moba[l].astype(BF16), wb=w_branch_diff[l].astype(BF16), wc=w_branch_nsa[l].astype(BF16),
        wout=w_out[l].astype(BF16),
    )


def _bias_tiles(rel_bias):
    bvec = rel_bias.T[:, _rel_bucket(jnp.arange(3 * TQ))]
    qi = jnp.arange(TQ)[:, None]
    ki = jnp.arange(TQ)[None, :]
    diag = jnp.where(qi >= ki, bvec[:, jnp.maximum(qi - ki, 0)], NEG)
    prev = bvec[:, TQ + qi - ki]
    far = bvec[:, 2 * TQ + qi - ki]
    return jnp.stack([diag, prev, far], axis=1).astype(F32)


def _const_tables(seq):
    blocks = np.arange(seq // NSA_CMP_STRIDE) * NSA_CMP_STRIDE
    lo = np.arange(seq // NSA_SEL_BLOCK) * NSA_SEL_BLOCK
    overlap = ((blocks[:, None] < lo[None, :] + NSA_SEL_BLOCK) & (blocks[:, None] + NSA_CMP_BLOCK > lo[None, :]))
    expand = np.arange(seq // NSA_SEL_BLOCK)[:, None] == (np.arange(seq)[None, :] // NSA_SEL_BLOCK)
    gexp = np.zeros((128, 768), np.float32)
    for k in range(3):
        for h in range(NSA_HEADS):
            gexp[k * NSA_HEADS + h, 256 * k + HEAD_DIM * h:256 * k + HEAD_DIM * (h + 1)] = 1.0
    bd = np.kron(np.eye(4, dtype=np.float32), np.ones((HEAD_DIM, HEAD_DIM), np.float32))
    return (jnp.asarray(overlap, F32), jnp.asarray(expand, BF16), jnp.asarray(gexp), jnp.asarray(bd, BF16))


def kernel(x_prompt, x_sample, cache_moba_kv, cache_diff_kv, cache_nsa_kv, state_nsa_win, page_table, rel_bias, norm_gain, ffn_w1, ffn_w2, w_in, qk_gain, diff_lambda, diff_subln, cmp_pos, cmp_w1, cmp_w2, w_branch_moba, w_branch_diff, w_branch_nsa, w_out):
    bsz, seq, _ = x_prompt.shape
    db, dq_len, _ = x_sample.shape
    n_p = bsz * seq
    n_s = db * dq_len
    assert dq_len == 1 and OWN_BLOCK >= MOBA_TOPK and CUR_SEL + 1 >= NSA_SEL_TOPK
    bias_tiles = _bias_tiles(rel_bias)
    overlap, expand, gexp, bd = _const_tables(seq)
    n_pool = cache_diff_kv.shape[1]
    moba_pages = jnp.transpose(cache_moba_kv, (0, 1, 3, 4, 5, 2))
    diff_pages = cache_diff_kv.reshape(DEPTH, n_pool, PAGE_SIZE * 8, 128)
    nsa_pages = jnp.transpose(cache_nsa_kv, (0, 1, 3, 4, 2))
    win_state = jnp.transpose(state_nsa_win, (0, 1, 3, 4, 2))
    bias_past, bias_nsa, bwin, b0, b0n = _decode_bias(rel_bias, win_state.shape[-1])
    ovl_dec = _decode_overlap()

    xp = x_prompt.reshape(n_p, D_MODEL)
    xs = x_sample.reshape(n_s, D_MODEL)
    outs = [[] for _ in range(8)]
    for l in range(DEPTH):
        lam_init = 0.8 - 0.6 * math.exp(-0.3 * l)
        pk = _pack_layer(l, norm_gain, ffn_w1, ffn_w2, w_in, qk_gain, cmp_pos, cmp_w1, cmp_w2,
                         w_branch_moba, w_branch_diff, w_branch_nsa, w_out)
        xp = _ffn(xp, pk['g0'], *pk['ffn1'], tm=1024)
        xs = _ffn(xs, pk['g0'], *pk['ffn1'], tm=n_s)

        mq, m_row, dq, d_row, nq, n_row, w_row, ngate = _inproj(
            xp, pk['g1'], pk['w_pack'], pk['gain_row'], pk['nmask_row'], bd, tm=256)
        r3 = lambda a: a.reshape(bsz, seq, a.shape[-1])
        mq, m_row, dq, d_row, nq, n_row, w_row = map(r3, (mq, m_row, dq, d_row, nq, n_row, w_row))
        o_a = _moba_prompt(mq, m_row, _moba_kmean(m_row), bias_tiles)
        o_b = _diff_prompt(dq, d_row, bias_tiles, diff_lambda[l], diff_subln[l][None, :], lam_init)
        kc4, vc4 = _nsa_compress(n_row[:, :, :128], pk['w1cat'], pk['pos_flat'], pk['w1raw'], pk['w2x4'], pk['gkc4'])
        o_cmp, selmask = _nsa_cmp_select(nq, kc4, vc4, overlap)
        o_sel = _nsa_selected(nq, n_row, selmask, expand, bias_tiles)
        o_win = _nsa_window(nq, w_row, bias_tiles)
        f2 = lambda a: a.reshape(n_p, a.shape[-1])
        xp = _merge(xp, pk['g1'], f2(o_a), f2(o_b), f2(o_cmp), f2(o_sel), f2(o_win), ngate,
                    pk['wbg'], pk['wa'], pk['wb'], pk['wc'], pk['wout'], gexp, tm=256)
        outs[0].append(m_row.reshape(bsz, seq, 2, MOBA_HEADS, HEAD_DIM))
        outs[2].append(d_row.reshape(bsz, seq, 2, DIFF_HEADS, DIFF_VDIM))
        outs[4].append(n_row.reshape(bsz, seq, 4, HEAD_DIM))
        outs[6].append(w_row[:, seq - min(NSA_WINDOW, seq):].reshape(bsz, min(NSA_WINDOW, seq), 2, HEAD_DIM))

        mq, m_row, dq, d_row, nq, n_row, w_row, ngate = _inproj(
            xs, pk['g1'], pk['w_pack'], pk['gain_row'], pk['nmask_row'], bd, tm=n_s)
        o_a = _moba_dec(l, mq, m_row, _moba_gate(l, mq, moba_pages, page_table), moba_pages, page_table,
                        bias_past, b0)
        o_b = _diff_dec(l, dq, d_row, diff_pages, page_table, bias_past, b0, diff_lambda[l],
                        diff_subln[l][None, :], lam_init)
        o_cmp, nsa_sel = _nsa_cmp_dec(l, nq, nsa_pages, page_table, pk['w1cat'].reshape(NSA_CMP_STRIDE, 128, 512)
                                      .astype(BF16), pk['pos_flat'], pk['w1raw'], cmp_w2[l], qk_gain[l, 5][None, :],
                                      ovl_dec)
        o_sel, o_win, buf = _nsa_dec(l, nq, n_row, w_row, nsa_sel, nsa_pages, win_state, page_table,
                                     bias_nsa, bwin, b0n)
        xs = _merge(xs, pk['g1'], o_a, o_b, o_cmp, o_sel, o_win, ngate,
                    pk['wbg'], pk['wa'], pk['wb'], pk['wc'], pk['wout'], gexp, tm=n_s)
        outs[1].append(m_row.reshape(db, dq_len, 2, MOBA_HEADS, HEAD_DIM))
        outs[3].append(d_row.reshape(db, dq_len, 2, DIFF_HEADS, DIFF_VDIM))
        outs[5].append(n_row.reshape(db, dq_len, 4, HEAD_DIM))
        outs[7].append(buf)

        xp = _ffn(xp, pk['g2'], *pk['ffn2'], tm=1024)
        xs = _ffn(xs, pk['g2'], *pk['ffn2'], tm=n_s)
    return (xp.reshape(bsz, seq, D_MODEL), xs.reshape(db, dq_len, D_MODEL)) + tuple(jnp.stack(o) for o in outs)
```

```python
import functools
import math

import numpy as np
import jax
import jax.numpy as jnp
from jax import lax
from jax.experimental import pallas as pl
from jax.experimental.pallas import tpu as pltpu

F32 = jnp.float32
BF16 = jnp.bfloat16
HI = lax.Precision.HIGHEST

D_MODEL = 1024
DEPTH = 2
PAST_LEN = 16384
PAGE_SIZE = 128
HEAD_DIM = 64
MOBA_HEADS = 4
MOBA_BLOCK = 256
MOBA_TOPK = 3
DIFF_HEADS = 4
DIFF_VDIM = 2 * HEAD_DIM
NSA_HEADS = 4
NSA_CMP_BLOCK = 32
NSA_CMP_STRIDE = 16
NSA_CMP_HIDDEN = 128
NSA_SEL_BLOCK = 64
NSA_SEL_TOPK = 16
NSA_WINDOW = 512
N_BUCKETS = 32
MAX_DISTANCE = 128
D_FF = 2688
EPS = 1e-6
NEG = -1e30
FORCE_BONUS = 1e4
SCALE = HEAD_DIM ** -0.5

TQ = 256
N_MAIN = 2944
N_PACK = 3072
VMEM_LIMIT = 56 * 1024 * 1024


def _cparams(sem):
    return pltpu.CompilerParams(dimension_semantics=sem, vmem_limit_bytes=VMEM_LIMIT)


def _rms(x, g):
    return x * lax.rsqrt(jnp.mean(x * x, axis=-1, keepdims=True) + EPS) * g


def _dot_nt(a, b, precision=None):
    return lax.dot_general(a, b, (((1,), (1,)), ((), ())), preferred_element_type=F32, precision=precision)


def _ffn_kernel(x_ref, g_ref, w1a_ref, w1b_ref, w2_ref, o_ref, u_scr, acc_scr):
    f = pl.program_id(1)

    @pl.when(f == 0)
    def _():
        u_scr[...] = _rms(x_ref[...], g_ref[...]).astype(BF16)
        acc_scr[...] = jnp.zeros_like(acc_scr)

    u = u_scr[...]
    a = jnp.dot(u, w1a_ref[...], preferred_element_type=F32)
    b = jnp.dot(u, w1b_ref[...], preferred_element_type=F32)
    h = (a * jax.nn.sigmoid(a) * b).astype(BF16)
    acc_scr[...] += jnp.dot(h, w2_ref[...], preferred_element_type=F32)

    @pl.when(f == pl.num_programs(1) - 1)
    def _():
        o_ref[...] = x_ref[...] + 0.5 * acc_scr[...]


def _ffn(x, g, w1, w2, tm):
    n = x.shape[0]
    tf = 896
    nf = D_FF // tf
    return pl.pallas_call(
        _ffn_kernel,
        grid=(n // tm, nf),
        in_specs=[
            pl.BlockSpec((tm, D_MODEL), lambda i, f: (i, 0)),
            pl.BlockSpec((1, D_MODEL), lambda i, f: (0, 0)),
            pl.BlockSpec((D_MODEL, tf), lambda i, f: (0, f)),
            pl.BlockSpec((D_MODEL, tf), lambda i, f: (0, f + nf)),
            pl.BlockSpec((tf, D_MODEL), lambda i, f: (f, 0)),
        ],
        out_specs=pl.BlockSpec((tm, D_MODEL), lambda i, f: (i, 0)),
        out_shape=jax.ShapeDtypeStruct((n, D_MODEL), F32),
        scratch_shapes=[pltpu.VMEM((tm, D_MODEL), BF16), pltpu.VMEM((tm, D_MODEL), F32)],
        compiler_params=_cparams(("parallel", "arbitrary")),
        name="ffn",
    )(x, g, w1, w1, w2)


_SLABS = (
    (0, 256, 0, 0),
    (256, 256, 1, 0),
    (512, 256, 1, 256),
    (768, 256, 2, 0),
    (1024, 256, 2, 256),
    (1280, 256, 3, 0),
    (1536, 256, 3, 256),
    (1792, 256, 3, 512),
    (2048, 256, 3, 768),
    (2304, 256, 4, 0),
    (2560, 256, 5, 0),
    (2816, 128, 6, 0),
    (2944, 128, 7, 0),
)


def _inproj_kernel(x_ref, g_ref, w_ref, gain_ref, nmask_ref, bd_ref,
                   mq_ref, mrow_ref, dq_ref, drow_ref, nq_ref, nrow_ref, wrow_ref, ngate_ref):
    outs = (mq_ref, mrow_ref, dq_ref, drow_ref, nq_ref, nrow_ref, wrow_ref, ngate_ref)
    u = _rms(x_ref[...], g_ref[...]).astype(BF16)
    for c0, w, oi, oc in _SLABS:
        z = jnp.dot(u, w_ref[:, c0:c0 + w], preferred_element_type=F32)
        if c0 < N_MAIN and c0 not in (512, 1792, 2048):
            ss = jnp.dot((z * z).astype(BF16), bd_ref[0:w, 0:w], preferred_element_type=F32)
            zn = z * lax.rsqrt(ss * (1.0 / HEAD_DIM) + EPS) * gain_ref[:, c0:c0 + w]
            z = jnp.where(nmask_ref[:, c0:c0 + w] > 0.5, zn, z)
        outs[oi][:, oc:oc + w] = z


def _inproj(x, g, w_pack, gain_row, nmask_row, bd, tm):
    n = x.shape[0]
    widths = (256, 512, 512, 1024, 256, 256, 128, 128)
    const = lambda i: (0, 0)
    return pl.pallas_call(
        _inproj_kernel,
        grid=(n // tm,),
        in_specs=[
            pl.BlockSpec((tm, D_MODEL), lambda i: (i, 0)),
            pl.BlockSpec((1, D_MODEL), const),
            pl.BlockSpec((D_MODEL, N_PACK), const),
            pl.BlockSpec((1, N_MAIN), const),
            pl.BlockSpec((1, N_MAIN), const),
            pl.BlockSpec((256, 256), const),
        ],
        out_specs=[pl.BlockSpec((tm, w), lambda i: (i, 0)) for w in widths],
        out_shape=[jax.ShapeDtypeStruct((n, w), F32) for w in widths],
        compiler_params=_cparams(("parallel",)),
        name="inproj",
    )(x, g, w_pack, gain_row, nmask_row, bd)


def _tri_tables(nq, depth=None):
    ti, tj = [], []
    for i in range(nq):
        lo = 0 if depth is None else max(0, i - depth + 1)
        for j in range(i, lo - 1, -1):
            ti.append(i)
            tj.append(j)
    return jnp.asarray(ti, jnp.int32), jnp.asarray(tj, jnp.int32)


def _softmax_update(s, m_ref, l_ref, narrow=True):
    m_old = m_ref[...]
    m_new = jnp.maximum(m_old, jnp.max(s, axis=-1, keepdims=True))
    alpha = jnp.exp(m_old - m_new)
    p = jnp.exp(s - m_new)
    l_ref[...] = alpha * l_ref[...] + jnp.sum(p, axis=-1, keepdims=True)
    m_ref[...] = m_new
    return alpha, (p.astype(BF16) if narrow else p)


def _softmax_lanes(s, m_ref):
    m_old = m_ref[...]
    m_new = jnp.maximum(m_old, jnp.max(s, axis=-1, keepdims=True))
    alpha = jnp.exp(m_old - m_new)
    p = jnp.exp(s - _lanes(m_new, s.shape[1]))
    m_ref[...] = m_new
    return alpha, p.astype(BF16)


def _lanes(x, width):
    return x if width == 128 else jnp.concatenate([x] * (width // 128), axis=1)


def _init_max_acc(m_ref, acc_ref):
    m_ref[...] = jnp.full(m_ref.shape, NEG, F32)
    acc_ref[...] = jnp.zeros(acc_ref.shape, F32)


def _rows(x, c, n=1):
    return x[TQ * c:TQ * (c + n)]


def _init_state(m_ref, l_ref, acc_ref):
    m_ref[...] = jnp.full(m_ref.shape, NEG, F32)
    l_ref[...] = jnp.zeros(l_ref.shape, F32)
    acc_ref[...] = jnp.zeros(acc_ref.shape, F32)


def _diff_kernel(ti_ref, tj_ref, q_ref, k_ref, v_ref, bias_ref, lam_ref, subln_ref, o_ref,
                 qs_ref, m_ref, acc_ref, *, lam_init):
    t = pl.program_id(1)
    i = ti_ref[t]
    j = tj_ref[t]

    @pl.when(j == i)
    def _():
        q = q_ref[0] * SCALE
        chunk = lax.broadcasted_iota(jnp.int32, (1, 256), 1) // HEAD_DIM
        for c in range(8):
            half = q[:, 256 * (c // 4):256 * (c // 4) + 256]
            qs_ref[TQ * c:TQ * (c + 1)] = jnp.where(chunk == c % 4, half, 0.0).astype(BF16)
        _init_max_acc(m_ref, acc_ref)

    kt = k_ref[0].astype(BF16)
    vt = v_ref[0].astype(BF16)
    tsel = jnp.minimum(i - j, 2)
    logits = []
    for h in range(DIFF_HEADS):
        bias = bias_ref[h, tsel]
        sh = _dot_nt(_rows(qs_ref, 2 * h, 2), kt[:, 256 * (h // 2):256 * (h // 2) + 256])
        logits += [_rows(sh, 0) + bias, _rows(sh, 1) + bias]
    alpha, p = _softmax_lanes(jnp.concatenate(logits, axis=0), m_ref)
    ones = jnp.ones((TQ, 128), BF16)
    pv = [jnp.dot(_rows(p, 2 * h, 2), jnp.concatenate([vt[:, DIFF_VDIM * h:DIFF_VDIM * (h + 1)], ones], axis=1),
                  preferred_element_type=F32) for h in range(DIFF_HEADS)]
    acc_ref[...] = _lanes(alpha, 256) * acc_ref[...] + jnp.concatenate(pv, axis=0)

    @pl.when(j == 0)
    def _():
        lam = lam_ref[...]
        lam_full = (jnp.exp(jnp.sum(lam[0:1] * lam[1:2], axis=-1, keepdims=True))
                    - jnp.exp(jnp.sum(lam[2:3] * lam[3:4], axis=-1, keepdims=True)) + lam_init)
        om = acc_ref[:, 0:DIFF_VDIM] / acc_ref[:, DIFF_VDIM:2 * DIFF_VDIM]
        for h in range(DIFF_HEADS):
            o = _rms(_rows(om, 2 * h) - lam_full * _rows(om, 2 * h + 1), subln_ref[...]) * (1.0 - lam_init)
            o_ref[0, :, DIFF_VDIM * h:DIFF_VDIM * (h + 1)] = o


def _diff_prompt(dq, d_row, bias_tiles, lam, subln, lam_init):
    b, s, _ = dq.shape
    ti, tj = _tri_tables(s // TQ)
    grid_spec = pltpu.PrefetchScalarGridSpec(
        num_scalar_prefetch=2,
        grid=(b, int(ti.shape[0])),
        in_specs=[
            pl.BlockSpec((1, TQ, 512), lambda b_, t, ti_, tj_: (b_, ti_[t], 0)),
            pl.BlockSpec((1, TQ, 512), lambda b_, t, ti_, tj_: (b_, tj_[t], 0)),
            pl.BlockSpec((1, TQ, 512), lambda b_, t, ti_, tj_: (b_, tj_[t], 1)),
            pl.BlockSpec((4, 3, TQ, TQ), lambda b_, t, ti_, tj_: (1, 0, 0, 0)),
            pl.BlockSpec((4, HEAD_DIM), lambda b_, t, ti_, tj_: (0, 0)),
            pl.BlockSpec((1, DIFF_VDIM), lambda b_, t, ti_, tj_: (0, 0)),
        ],
        out_specs=pl.BlockSpec((1, TQ, 512), lambda b_, t, ti_, tj_: (b_, ti_[t], 0)),
        scratch_shapes=[
            pltpu.VMEM((8 * TQ, 256), BF16),
            pltpu.VMEM((8 * TQ, 128), F32),
            pltpu.VMEM((8 * TQ, 2 * DIFF_VDIM), F32),
        ],
    )
    return pl.pallas_call(
        functools.partial(_diff_kernel, lam_init=lam_init),
        grid_spec=grid_spec,
        out_shape=jax.ShapeDtypeStruct((b, s, 512), F32),
        compiler_params=_cparams(("parallel", "arbitrary")),
        name="diff_prompt",
    )(ti, tj, dq, d_row, d_row, bias_tiles, lam, subln)


def _rank_desc(score, n):
    axis = score.ndim - 1
    col = lax.broadcasted_iota(jnp.int32, (1,) * axis + (score.shape[-1],), axis)
    rank = jnp.zeros(score.shape, F32)
    for jp in range(n):
        sj = score[..., jp:jp + 1]
        tie = jnp.where(jp < col, 1.0, 0.0)
        rank = rank + jnp.where(sj > score, 1.0, jnp.where(sj == score, tie, 0.0))
    return rank


def _kmean_kernel(k_ref, o_ref, *, nb):
    o_ref[0] = jnp.zeros(o_ref.shape[1:], F32)
    o_ref[0, 0:nb] = jnp.mean(k_ref[0].reshape(nb, MOBA_BLOCK, 256), axis=1)


def _moba_kmean(m_row):
    b, s, _ = m_row.shape
    nb = s // MOBA_BLOCK
    return pl.pallas_call(
        functools.partial(_kmean_kernel, nb=nb),
        grid=(b,),
        in_specs=[pl.BlockSpec((1, s, 256), lambda i: (i, 0, 0))],
        out_specs=pl.BlockSpec((1, 128, 256), lambda i: (i, 0, 0)),
        out_shape=jax.ShapeDtypeStruct((b, 128, 256), F32),
        compiler_params=_cparams(("parallel",)),
        name="moba_kmean",
    )(m_row)


def _moba_kernel(ti_ref, tj_ref, q_ref, k_ref, v_ref, kmean_ref, bias_ref, spread_ref, o_ref,
                 qs_ref, sel_ref, m_ref, acc_ref, *, nb):
    t = pl.program_id(1)
    i = ti_ref[t]
    j = tj_ref[t]
    head = lax.broadcasted_iota(jnp.int32, (1, 256), 1) // HEAD_DIM
    col = lax.broadcasted_iota(jnp.int32, (1, 128), 1)

    @pl.when(j == i)
    def _():
        q = q_ref[0]
        qh = jnp.concatenate([jnp.where(head == h, q, 0.0) for h in range(MOBA_HEADS)], axis=0)
        qs_ref[...] = (qh * SCALE).astype(BF16)
        gate = _dot_nt(qh, kmean_ref[0], precision=HI)
        valid = col < i
        gate = jnp.where(valid, gate, NEG)
        sel = valid & (_rank_desc(gate, nb) < MOBA_TOPK)
        wide = jnp.dot(jnp.where(sel, 1.0, 0.0).astype(BF16), spread_ref[...], preferred_element_type=F32)
        for jj in range(nb):
            sel_ref[jj] = wide[:, 128 * jj:128 * (jj + 1)]
        _init_max_acc(m_ref, acc_ref)

    kt = k_ref[0].astype(BF16)
    vt = v_ref[0].astype(BF16)
    tsel = jnp.minimum(i - j, 2)
    bias = jnp.concatenate([bias_ref[h, tsel] for h in range(MOBA_HEADS)], axis=0)
    ok = _lanes(sel_ref[j] + jnp.where(j == i, 1.0, 0.0), TQ) > 0.5
    s = jnp.where(ok, _dot_nt(qs_ref[...], kt) + bias, NEG)
    alpha, p = _softmax_lanes(s, m_ref)
    vx = jnp.concatenate([vt, jnp.ones((TQ, 128), BF16)], axis=1)
    acc_ref[...] = _lanes(alpha, 384) * acc_ref[...] + jnp.dot(p, vx, preferred_element_type=F32)

    @pl.when(j == 0)
    def _():
        om = acc_ref[:, 0:256] / _lanes(acc_ref[:, 256:384], 256)
        o = jnp.zeros((TQ, 256), F32)
        for h in range(MOBA_HEADS):
            o = jnp.where(head == h, _rows(om, h), o)
        o_ref[0] = o


def _moba_prompt(mq, m_row, kmean, bias_tiles):
    b, s, _ = mq.shape
    nb = s // MOBA_BLOCK
    ti, tj = _tri_tables(s // TQ)
    spread = jnp.asarray(np.kron(np.eye(128, nb, dtype=np.float32), np.ones((1, 128), np.float32)), BF16)
    grid_spec = pltpu.PrefetchScalarGridSpec(
        num_scalar_prefetch=2,
        grid=(b, int(ti.shape[0])),
        in_specs=[
            pl.BlockSpec((1, TQ, 256), lambda b_, t, ti_, tj_: (b_, ti_[t], 0)),
            pl.BlockSpec((1, TQ, 256), lambda b_, t, ti_, tj_: (b_, tj_[t], 0)),
            pl.BlockSpec((1, TQ, 256), lambda b_, t, ti_, tj_: (b_, tj_[t], 1)),
            pl.BlockSpec((1, 128, 256), lambda b_, t, ti_, tj_: (b_, 0, 0)),
            pl.BlockSpec((4, 3, TQ, TQ), lambda b_, t, ti_, tj_: (0, 0, 0, 0)),
            pl.BlockSpec((128, nb * 128), lambda b_, t, ti_, tj_: (0, 0)),
        ],
        out_specs=pl.BlockSpec((1, TQ, 256), lambda b_, t, ti_, tj_: (b_, ti_[t], 0)),
        scratch_shapes=[
            pltpu.VMEM((4 * TQ, 256), BF16),
            pltpu.VMEM((nb, 4 * TQ, 128), F32),
            pltpu.VMEM((4 * TQ, 128), F32),
            pltpu.VMEM((4 * TQ, 384), F32),
        ],
    )
    return pl.pallas_call(
        functools.partial(_moba_kernel, nb=nb),
        grid_spec=grid_spec,
        out_shape=jax.ShapeDtypeStruct((b, s, 256), F32),
        compiler_params=_cparams(("parallel", "arbitrary")),
        name="moba_prompt",
    )(ti, tj, mq, m_row, m_row, kmean, bias_tiles, spread)


def _compress_kernel(r_ref, w1_ref, pos_ref, w1raw_ref, w2_ref, gkc_ref, kc_ref, vc_ref):
    a = jnp.dot(r_ref[0], w1_ref[...], preferred_element_type=F32, precision=HI)
    n = a.shape[0]
    outs = (kc_ref, vc_ref)
    for kv in range(2):
        pos = jnp.broadcast_to(pos_ref[kv], (8, pos_ref.shape[-1]))
        cst = jnp.dot(pos, w1raw_ref[kv], preferred_element_type=F32, precision=HI)[0:1]
        top = a[:, 256 * kv:256 * kv + 128]
        bot = a[:, 256 * kv + 128:256 * kv + 256]
        hid = top + pltpu.roll(bot, n - 1, 0) + cst
        out = jnp.dot(jax.nn.gelu(hid), w2_ref[kv], preferred_element_type=F32, precision=HI)
        if kv == 0:
            out = out * lax.rsqrt(jnp.mean(out * out, axis=-1, keepdims=True) + EPS) * gkc_ref[...]
        outs[kv][0] = out


def _nsa_compress(ckv, w1cat, pos_flat, w1raw, w2x4, gkc4):
    b, s, _ = ckv.shape
    n = s // NSA_CMP_STRIDE
    r = ckv.reshape(b, n, NSA_CMP_STRIDE * 128)
    return pl.pallas_call(
        _compress_kernel,
        grid=(b,),
        in_specs=[
            pl.BlockSpec((1, n, NSA_CMP_STRIDE * 128), lambda i: (i, 0, 0)),
            pl.BlockSpec((NSA_CMP_STRIDE * 128, 512), lambda i: (0, 0)),
            pl.BlockSpec((2, 1, NSA_CMP_BLOCK * HEAD_DIM), lambda i: (0, 0, 0)),
            pl.BlockSpec((2, NSA_CMP_BLOCK * HEAD_DIM, NSA_CMP_HIDDEN), lambda i: (0, 0, 0)),
            pl.BlockSpec((2, NSA_CMP_HIDDEN, 256), lambda i: (0, 0, 0)),
            pl.BlockSpec((1, 256), lambda i: (0, 0)),
        ],
        out_specs=[pl.BlockSpec((1, n, 256), lambda i: (i, 0, 0))] * 2,
        out_shape=[jax.ShapeDtypeStruct((b, n, 256), F32)] * 2,
        compiler_params=_cparams(("parallel",)),
        name="nsa_compress",
    )(r, w1cat, pos_flat, w1raw, w2x4, gkc4)


def _nsa_cmp_kernel(q_ref, kc_ref, vc_ref, ovl_ref, o_ref, sel_ref):
    i = pl.program_id(1)
    ncmp = kc_ref.shape[1]
    nsel = ovl_ref.shape[1]
    q = q_ref[0]
    head = lax.broadcasted_iota(jnp.int32, (1, 256), 1) // HEAD_DIM
    tpos = i * TQ + lax.broadcasted_iota(jnp.int32, (TQ, 1), 0)
    blk_end = lax.broadcasted_iota(jnp.int32, (1, ncmp), 1) * NSA_CMP_STRIDE + (NSA_CMP_BLOCK - 1)
    valid = blk_end <= tpos
    vc = vc_ref[0].astype(BF16)
    imp = jnp.zeros((TQ, nsel), F32)
    o = jnp.zeros((TQ, 256), F32)
    for h in range(NSA_HEADS):
        qh = jnp.where(head == h, q, 0.0)
        s = jnp.where(valid, _dot_nt(qh, kc_ref[0], precision=HI) * SCALE, NEG)
        p = jnp.where(valid, jnp.exp(s - jnp.max(s, axis=-1, keepdims=True)), 0.0)
        p = p / jnp.maximum(jnp.sum(p, axis=-1, keepdims=True), 1e-30)
        o = jnp.where(head == h, jnp.dot(p.astype(BF16), vc, preferred_element_type=F32), o)
        imp = imp + jnp.dot(p, ovl_ref[...], preferred_element_type=F32, precision=HI)
    o_ref[0] = o
    jcol = lax.broadcasted_iota(jnp.int32, (1, nsel), 1)
    cur = tpos // NSA_SEL_BLOCK
    forced = (jcol == 0) | (jcol == cur) | (jcol == cur - 1)
    score = jnp.where(jcol <= cur, imp + jnp.where(forced, FORCE_BONUS, 0.0), NEG)
    sel = (jcol <= cur) & (_rank_desc(score, nsel) < NSA_SEL_TOPK)
    sel_ref[0] = jnp.where(sel, 1.0, 0.0)


def _nsa_cmp_select(nq, kc4, vc4, overlap):
    b, s, _ = nq.shape
    ncmp = kc4.shape[1]
    nsel = overlap.shape[1]
    return pl.pallas_call(
        _nsa_cmp_kernel,
        grid=(b, s // TQ),
        in_specs=[
            pl.BlockSpec((1, TQ, 256), lambda b_, i: (b_, i, 0)),
            pl.BlockSpec((1, ncmp, 256), lambda b_, i: (b_, 0, 0)),
            pl.BlockSpec((1, ncmp, 256), lambda b_, i: (b_, 0, 0)),
            pl.BlockSpec((ncmp, nsel), lambda b_, i: (0, 0)),
        ],
        out_specs=[pl.BlockSpec((1, TQ, 256), lambda b_, i: (b_, i, 0)),
                   pl.BlockSpec((1, TQ, nsel), lambda b_, i: (b_, i, 0))],
        out_shape=[jax.ShapeDtypeStruct((b, s, 256), F32), jax.ShapeDtypeStruct((b, s, nsel), F32)],
        compiler_params=_cparams(("parallel", "parallel")),
        name="nsa_cmp_select",
    )(nq, kc4, vc4, overlap)


def _load_q128(q_ref, qs_ref):
    q = q_ref[0] * SCALE
    low = lax.broadcasted_iota(jnp.int32, (1, 128), 1) < HEAD_DIM
    for h in range(NSA_HEADS):
        slab = q[:, 128 * (h // 2):128 * (h // 2) + 128]
        if h % 2:
            slab = pltpu.roll(slab, HEAD_DIM, 1)
        qs_ref[TQ * h:TQ * (h + 1)] = jnp.where(low, slab, 0.0).astype(BF16)


def _kv128_step(mask, kv, tsel, bias_ref, qs_ref, m_ref, acc_ref):
    dots = _dot_nt(qs_ref[...], kv)
    s = jnp.concatenate([jnp.where(mask, _rows(dots, h) + bias_ref[h, tsel], NEG) for h in range(NSA_HEADS)], axis=0)
    alpha, p = _softmax_lanes(s, m_ref)
    low = lax.broadcasted_iota(jnp.int32, (1, 128), 1) < HEAD_DIM
    acc_ref[...] = alpha * acc_ref[...] + jnp.dot(p, jnp.where(low, jnp.ones_like(kv), kv),
                                                  preferred_element_type=F32)


def _store_o128(acc_ref, o_ref):
    low = lax.broadcasted_iota(jnp.int32, (1, 128), 1) < HEAD_DIM
    acc = acc_ref[...]
    om = acc / pltpu.roll(acc, HEAD_DIM, 1)
    for pair in range(NSA_HEADS // 2):
        even = pltpu.roll(_rows(om, 2 * pair), HEAD_DIM, 1)
        o_ref[0, :, 128 * pair:128 * pair + 128] = jnp.where(low, even, _rows(om, 2 * pair + 1))


def _nsa_sel_kernel(ti_ref, tj_ref, q_ref, kv_ref, selm_ref, exp_ref, bias_ref, o_ref,
                    qs_ref, m_ref, acc_ref):
    t = pl.program_id(1)
    i = ti_ref[t]
    j = tj_ref[t]

    @pl.when(j == i)
    def _():
        _load_q128(q_ref, qs_ref)
        _init_max_acc(m_ref, acc_ref)

    kv = kv_ref[0].astype(BF16)
    picked = jnp.dot(selm_ref[0].astype(BF16), exp_ref[...], preferred_element_type=F32) > 0.5
    _kv128_step(picked, kv, jnp.minimum(i - j, 2), bias_ref, qs_ref, m_ref, acc_ref)

    @pl.when(j == 0)
    def _():
        _store_o128(acc_ref, o_ref)


def _nsa_win_kernel(ti_ref, tj_ref, q_ref, kv_ref, bias_ref, o_ref, qs_ref, m_ref, acc_ref):
    t = pl.program_id(1)
    i = ti_ref[t]
    j = tj_ref[t]

    @pl.when(j == i)
    def _():
        _load_q128(q_ref, qs_ref)
        _init_max_acc(m_ref, acc_ref)

    kv = kv_ref[0].astype(BF16)
    dist = ((i - j) * TQ + lax.broadcasted_iota(jnp.int32, (TQ, TQ), 0)
            - lax.broadcasted_iota(jnp.int32, (TQ, TQ), 1))
    _kv128_step(dist < NSA_WINDOW, kv, jnp.minimum(i - j, 2), bias_ref, qs_ref, m_ref, acc_ref)

    @pl.when((j == 0) | (i - j == NSA_WINDOW // TQ))
    def _():
        _store_o128(acc_ref, o_ref)


def _kv128_scratch():
    return [
        pltpu.VMEM((4 * TQ, 128), BF16),
        pltpu.VMEM((4 * TQ, 128), F32),
        pltpu.VMEM((4 * TQ, 128), F32),
    ]


def _nsa_selected(nq, n_row, selmask, expand, bias_tiles):
    b, s, _ = nq.shape
    nsel = selmask.shape[-1]
    ti, tj = _tri_tables(s // TQ)
    grid_spec = pltpu.PrefetchScalarGridSpec(
        num_scalar_prefetch=2,
        grid=(b, int(ti.shape[0])),
        in_specs=[
            pl.BlockSpec((1, TQ, 256), lambda b_, t, ti_, tj_: (b_, ti_[t], 0)),
            pl.BlockSpec((1, TQ, 128), lambda b_, t, ti_, tj_: (b_, tj_[t], 1)),
            pl.BlockSpec((1, TQ, nsel), lambda b_, t, ti_, tj_: (b_, ti_[t], 0)),
            pl.BlockSpec((nsel, TQ), lambda b_, t, ti_, tj_: (0, tj_[t])),
            pl.BlockSpec((4, 3, TQ, TQ), lambda b_, t, ti_, tj_: (2, 0, 0, 0)),
        ],
        out_specs=pl.BlockSpec((1, TQ, 256), lambda b_, t, ti_, tj_: (b_, ti_[t], 0)),
        scratch_shapes=_kv128_scratch(),
    )
    return pl.pallas_call(
        _nsa_sel_kernel,
        grid_spec=grid_spec,
        out_shape=jax.ShapeDtypeStruct((b, s, 256), F32),
        compiler_params=_cparams(("parallel", "arbitrary")),
        name="nsa_selected",
    )(ti, tj, nq, n_row, selmask, expand, bias_tiles)


def _nsa_window(nq, w_row, bias_tiles):
    b, s, _ = nq.shape
    ti, tj = _tri_tables(s // TQ, depth=NSA_WINDOW // TQ + 1)
    grid_spec = pltpu.PrefetchScalarGridSpec(
        num_scalar_prefetch=2,
        grid=(b, int(ti.shape[0])),
        in_specs=[
            pl.BlockSpec((1, TQ, 256), lambda b_, t, ti_, tj_: (b_, ti_[t], 0)),
            pl.BlockSpec((1, TQ, 128), lambda b_, t, ti_, tj_: (b_, tj_[t], 0)),
            pl.BlockSpec((4, 3, TQ, TQ), lambda b_, t, ti_, tj_: (2, 0, 0, 0)),
        ],
        out_specs=pl.BlockSpec((1, TQ, 256), lambda b_, t, ti_, tj_: (b_, ti_[t], 0)),
        scratch_shapes=_kv128_scratch(),
    )
    return pl.pallas_call(
        _nsa_win_kernel,
        grid_spec=grid_spec,
        out_shape=jax.ShapeDtypeStruct((b, s, 256), F32),
        compiler_params=_cparams(("parallel", "arbitrary")),
        name="nsa_window",
    )(ti, tj, nq, w_row, bias_tiles)


def _merge_kernel(x_ref, g_ref, oa_ref, ob_ref, ocmp_ref, osel_ref, owin_ref, ngate_ref,
                  wbg_ref, wa_ref, wb_ref, wc_ref, wout_ref, gexp_ref, o_ref):
    x = x_ref[...]
    u = _rms(x, g_ref[...]).astype(BF16)
    ng = jax.nn.sigmoid(ngate_ref[...])
    gx = jnp.dot(ng, gexp_ref[...], preferred_element_type=F32, precision=HI)
    oc = gx[:, 0:256] * ocmp_ref[...] + gx[:, 256:512] * osel_ref[...] + gx[:, 512:768] * owin_ref[...]
    branches = ((oa_ref[...], wa_ref), (ob_ref[...], wb_ref), (oc, wc_ref))
    hsum = jnp.zeros(x.shape, F32)
    for k, (ob, w_ref) in enumerate(branches):
        gate = jax.nn.sigmoid(jnp.dot(u, wbg_ref[:, D_MODEL * k:D_MODEL * (k + 1)], preferred_element_type=F32))
        hsum = hsum + gate * jnp.dot(ob.astype(BF16), w_ref[...], preferred_element_type=F32)
    o_ref[...] = x + jnp.dot(hsum.astype(BF16), wout_ref[...], preferred_element_type=F32)


def _merge(x, g, o_a, o_b, o_cmp, o_sel, o_win, ngate, wbg, wa, wb, wc, wout, gexp, tm):
    n = x.shape[0]
    row = lambda w: pl.BlockSpec((tm, w), lambda i: (i, 0))
    full = lambda a: pl.BlockSpec(a.shape, lambda i: (0,) * a.ndim)
    return pl.pallas_call(
        _merge_kernel,
        grid=(n // tm,),
        in_specs=[row(D_MODEL), full(g), row(256), row(512), row(256), row(256), row(256), row(128),
                  full(wbg), full(wa), full(wb), full(wc), full(wout), full(gexp)],
        out_specs=row(D_MODEL),
        out_shape=jax.ShapeDtypeStruct((n, D_MODEL), F32),
        compiler_params=_cparams(("parallel",)),
        name="merge",
    )(x, g, o_a, o_b, o_cmp, o_sel, o_win, ngate, wbg, wa, wb, wc, wout, gexp)


def _rel_bucket(dist):
    n = jnp.maximum(dist, 0)
    exact = N_BUCKETS // 2
    nf = jnp.maximum(n, 1).astype(F32)
    large = exact + (jnp.log(nf / exact) / math.log(MAX_DISTANCE / exact) * (N_BUCKETS - exact)).astype(jnp.int32)
    return jnp.where(n < exact, n, jnp.minimum(large, N_BUCKETS - 1))


def _head_bias(table, dist):
    hidx = jnp.arange(table.shape[1]).reshape(-1, 1, 1)
    return table.T[hidx, _rel_bucket(dist)].astype(F32)


def _rms_norm(x, g):
    xf = x.astype(F32)
    y = xf * lax.rsqrt(jnp.mean(xf * xf, axis=-1, keepdims=True) + EPS)
    return (y * g.astype(F32)).astype(x.dtype)


def _masked_softmax(logits, mask):
    logits = jnp.where(mask, logits, NEG)
    m = jnp.max(logits, axis=-1, keepdims=True)
    p = jnp.where(mask, jnp.exp(logits - m), 0.0)
    return p / jnp.maximum(jnp.sum(p, axis=-1, keepdims=True), 1e-30)


def _gather_rows(pool_l, page_table, new_rows, pos, head=None):
    b = jnp.arange(pos.shape[0]).reshape((-1,) + (1,) * (pos.ndim - 1))
    pc = jnp.clip(pos, 0, PAST_LEN - 1)
    phys = page_table[b, pc // PAGE_SIZE]
    pn = jnp.clip(pos - PAST_LEN, 0, new_rows.shape[1] - 1)
    if head is None:
        past = pool_l[phys, pc % PAGE_SIZE]
        new = new_rows[b, pn]
    else:
        past = pool_l[phys, pc % PAGE_SIZE, :, head]
        new = new_rows[b, pn, :, head]
    is_past = (pos < PAST_LEN).reshape(pos.shape + (1,) * (past.ndim - pos.ndim))
    return jnp.where(is_past, past, new)


def _block_means(k, nb):
    bsz, length, h, d = k.shape
    kb = jnp.pad(k, ((0, 0), (0, nb * MOBA_BLOCK - length), (0, 0), (0, 0))).reshape(bsz, nb, MOBA_BLOCK, h, d)
    return jnp.mean(kb.astype(F32), axis=2), kb


def _moba_select(q, kmean, own):
    nb = kmean.shape[1]
    gate = jnp.einsum('bqhd,bnhd->bhqn', q, kmean, preferred_element_type=F32)
    gate = jnp.where(jnp.arange(nb)[None, :] < own[:, None], gate, NEG)
    _, sel = lax.top_k(gate, min(MOBA_TOPK, nb))
    return sel


def _moba_decode(q, m_new, pool_l, page_table, table):
    db, qn, h, d = q.shape
    length = PAST_LEN + qn
    t = PAST_LEN + jnp.arange(qn)
    k_past = pool_l[page_table, :, 0].reshape(db, PAST_LEN, h, d)
    kmean, _ = _block_means(jnp.concatenate([k_past, m_new[:, :, 0]], axis=1), -(-length // MOBA_BLOCK))
    own = t // MOBA_BLOCK
    sel = _moba_select(q, kmean, own)
    n = sel.shape[-1]
    ar = jnp.arange(MOBA_BLOCK)
    pos_sel = (sel[..., None] * MOBA_BLOCK + ar).reshape(db, h, qn, n * MOBA_BLOCK)
    ok_sel = jnp.broadcast_to((sel < own[:, None])[..., None], sel.shape + (MOBA_BLOCK,)).reshape(pos_sel.shape)
    pos_own = own[:, None] * MOBA_BLOCK + ar
    pos = jnp.concatenate([pos_sel, jnp.broadcast_to(pos_own, (db, h, qn, MOBA_BLOCK))], -1)
    mask = jnp.concatenate([ok_sel, jnp.broadcast_to(pos_own <= t[:, None], (db, h, qn, MOBA_BLOCK))], -1)
    kv = _gather_rows(pool_l, page_table, m_new, pos, jnp.arange(h).reshape(1, h, 1, 1))
    logits = jnp.einsum('bqhd,bhqkd->bhqk', q, kv[..., 0, :], preferred_element_type=F32) * SCALE
    p = _masked_softmax(logits + _head_bias(table, t[:, None] - pos), mask)
    o = jnp.einsum('bhqk,bhqkd->bqhd', p, kv[..., 1, :])
    return o.reshape(db, qn, h * d)


def _diff_combine(om, lam, lam_init, subln):
    lamf = lam.astype(F32)
    lam_full = jnp.exp(jnp.sum(lamf[0] * lamf[1])) - jnp.exp(jnp.sum(lamf[2] * lamf[3])) + lam_init
    o = om[..., 0, :] - lam_full * om[..., 1, :]
    o = _rms_norm(o, subln) * (1.0 - lam_init)
    return o.reshape(o.shape[:-2] + (-1,))


def _diff_decode(q, d_row, pool_l, page_table, table, lam, lam_init, subln):
    db, qn, h, _, d = q.shape
    t = PAST_LEN + jnp.arange(qn)

    def update(carry, k, v, pos):
        m, s, acc = carry
        logits = jnp.einsum('bqhmd,bkhmd->bhmqk', q, k, preferred_element_type=F32) * SCALE
        logits = logits + _head_bias(table, t[:, None] - pos[None, :])[:, None]
        mask = pos[None, :] <= t[:, None]
        logits = jnp.where(mask, logits, NEG)
        m_new = jnp.maximum(m, jnp.max(logits, axis=-1))
        p = jnp.where(mask, jnp.exp(logits - m_new[..., None]), 0.0)
        corr = jnp.exp(m - m_new)
        s = s * corr + jnp.sum(p, axis=-1)
        acc = acc * corr[..., None] + jnp.einsum('bhmqk,bkhe->bhmqe', p, v, preferred_element_type=F32)
        return (m_new, s, acc)

    def page_step(carry, xs_):
        j, phys = xs_
        kv = pool_l[phys]
        k = kv[:, :, 0].reshape(db, PAGE_SIZE, h, 2, d)
        return update(carry, k, kv[:, :, 1], j * PAGE_SIZE + jnp.arange(PAGE_SIZE)), None

    init = (jnp.full((db, h, 2, qn), NEG, F32), jnp.zeros((db, h, 2, qn), F32),
            jnp.zeros((db, h, 2, qn, DIFF_VDIM), F32))
    carry, _ = lax.scan(page_step, init, (jnp.arange(page_table.shape[1]), page_table.T))
    _, s, acc = update(carry, d_row[:, :, 0].reshape(db, qn, h, 2, d), d_row[:, :, 1], t)
    om = jnp.moveaxis(acc / s[..., None], 3, 1)
    return _diff_combine(om, lam, lam_init, subln)


def _compress(rows, pos_emb, w1, w2):
    bsz, length, d = rows.shape
    n_cmp = (length - NSA_CMP_BLOCK) // NSA_CMP_STRIDE + 1
    idx = jnp.arange(n_cmp)[:, None] * NSA_CMP_STRIDE + jnp.arange(NSA_CMP_BLOCK)[None, :]
    blk = (rows[:, idx] + pos_emb).reshape(bsz, n_cmp, NSA_CMP_BLOCK * d)
    return jax.nn.gelu(blk @ w1) @ w2


def _nsa_compressed(q, ck, cv, t, g_kc, cmp_pos, cmp_w1, cmp_w2):
    kc = _rms_norm(_compress(ck, cmp_pos[0], cmp_w1[0], cmp_w2[0]), g_kc)
    vc = _compress(cv, cmp_pos[1], cmp_w1[1], cmp_w2[1])
    starts = jnp.arange(kc.shape[1]) * NSA_CMP_STRIDE
    logits = jnp.einsum('bqhd,bnd->bhqn', q, kc, preferred_element_type=F32) * SCALE
    p = _masked_softmax(logits, (starts + NSA_CMP_BLOCK - 1)[None, :] <= t[:, None])
    o = jnp.einsum('bhqn,bnd->bqhd', p, vc)
    lo = jnp.arange(-(-ck.shape[1] // NSA_SEL_BLOCK)) * NSA_SEL_BLOCK
    overlap = ((starts[:, None] < lo[None, :] + NSA_SEL_BLOCK)
               & (starts[:, None] + NSA_CMP_BLOCK > lo[None, :])).astype(F32)
    imp = jnp.einsum('bhqn,nj->bqj', p, overlap)
    return o, imp


def _nsa_select_blocks(imp, t):
    j = jnp.arange(imp.shape[-1])[None, :]
    cur = (t // NSA_SEL_BLOCK)[:, None]
    forced = (j == 0) | (j == cur) | (j == cur - 1)
    score = jnp.where(j <= cur, imp + FORCE_BONUS * forced, NEG)
    _, idx = lax.top_k(score, min(NSA_SEL_TOPK, imp.shape[-1]))
    return idx


def _nsa_sel_attend(q, ks, vs, pos, t, table):
    dist = (t[None, :, None] - pos)[:, None]
    logits = jnp.einsum('bqhd,bqkd->bhqk', q, ks, preferred_element_type=F32) * SCALE
    p = _masked_softmax(logits + _head_bias(table, dist), dist >= 0)
    return jnp.einsum('bhqk,bqkd->bqhd', p, vs)


def _nsa_window_decode(q, buf, w_new, t, table):
    w_buf = buf.shape[1]
    kv = jnp.concatenate([buf, w_new], axis=1)
    pos = PAST_LEN - w_buf + jnp.arange(kv.shape[1])
    dist = t[:, None] - pos[None, :]
    logits = jnp.einsum('bqhd,bkd->bhqk', q, kv[:, :, 0], preferred_element_type=F32) * SCALE
    p = _masked_softmax(logits + _head_bias(table, dist), (dist >= 0) & (dist < NSA_WINDOW))
    o = jnp.einsum('bhqk,bkd->bqhd', p, kv[:, :, 1])
    return o, kv[:, kv.shape[1] - w_buf:]


def _nsa_decode(q, n_new, w_new, pool_l, page_table, buf, table, g_kc, cmp_pos, cmp_w1, cmp_w2):
    db, qn, h, d = q.shape
    t = PAST_LEN + jnp.arange(qn)
    past_c = pool_l[page_table, :, :2].reshape(db, PAST_LEN, 2, d)
    all_c = jnp.concatenate([past_c, n_new[:, :, :2]], axis=1)
    o_cmp, imp = _nsa_compressed(q, all_c[:, :, 0], all_c[:, :, 1], t, g_kc, cmp_pos, cmp_w1, cmp_w2)
    idx = _nsa_select_blocks(imp, t)
    pos = (idx[..., None] * NSA_SEL_BLOCK + jnp.arange(NSA_SEL_BLOCK)).reshape(db, qn, -1)
    rows = _gather_rows(pool_l, page_table, n_new, pos)
    o_sel = _nsa_sel_attend(q, rows[..., 2, :], rows[..., 3, :], pos, t, table)
    o_win, new_buf = _nsa_window_decode(q, buf, w_new, t, table)
    return o_cmp, o_sel, o_win, new_buf


PAGES_PER_STEP = 16
N_PAGES = PAST_LEN // PAGE_SIZE
OWN_BLOCK = PAST_LEN // MOBA_BLOCK
CUR_SEL = PAST_LEN // NSA_SEL_BLOCK
IMP_W = 384


def _page_of(pt, b, slot, n_pool):
    b = jnp.clip(b, 0, pt.shape[0] - 1)
    return jnp.clip(pt[b, jnp.clip(slot, 0, N_PAGES - 1)], 0, n_pool - 1)


def _picked(sl, b, k):
    return sl[jnp.clip(b, 0, sl.shape[0] - 1), k]


def _row_of_heads(x, n):
    return jnp.concatenate([x[h:h + 1] for h in range(n)], axis=1)


def _diff_dec_kernel(pt_ref, q_ref, knew_ref, vnew_ref, bias_ref, b0_ref, lam_ref, subln_ref, *rest, lam_init):
    pages = rest[:PAGES_PER_STEP]
    o_ref, m_ref, l_ref, acc_ref = rest[PAGES_PER_STEP:]
    s = pl.program_id(1)

    @pl.when(s == 0)
    def _():
        for h in range(DIFF_HEADS):
            m_ref[8 * h:8 * (h + 1)] = (jnp.sum(q_ref[0, h] * knew_ref[0, h], axis=-1, keepdims=True)
                                        + b0_ref[h][:, 0:1])
            acc_ref[8 * h:8 * (h + 1)] = jnp.broadcast_to(vnew_ref[0, h], (8, DIFF_VDIM))
        l_ref[...] = jnp.ones(l_ref.shape, F32)

    logits = []
    for h in range(DIFF_HEADS):
        qh = q_ref[0, h].astype(BF16)
        logits.append(jnp.concatenate(
            [_dot_nt(qh, pages[r][pl.ds(h, PAGE_SIZE, stride=8), :].astype(BF16)) + bias_ref[h, r:r + 1, :]
             for r in range(PAGES_PER_STEP)], axis=1))
    alpha, p = _softmax_update(jnp.concatenate(logits, axis=0), m_ref, l_ref, narrow=False)
    pvs = []
    for h in range(DIFF_HEADS):
        pv = jnp.zeros((8, DIFF_VDIM), F32)
        for r in range(PAGES_PER_STEP):
            vh = pages[r][pl.ds(DIFF_HEADS + h, PAGE_SIZE, stride=8), :].astype(BF16)
            pv = pv + jnp.dot(p[8 * h:8 * (h + 1), PAGE_SIZE * r:PAGE_SIZE * (r + 1)].astype(BF16), vh,
                              preferred_element_type=F32)
        pvs.append(pv)
    acc_ref[...] = alpha * acc_ref[...] + jnp.concatenate(pvs, axis=0)

    @pl.when(s == pl.num_programs(1) - 1)
    def _():
        lam = lam_ref[...]
        lam_full = (jnp.exp(jnp.sum(lam[0:1] * lam[1:2], axis=-1, keepdims=True))
                    - jnp.exp(jnp.sum(lam[2:3] * lam[3:4], axis=-1, keepdims=True)) + lam_init)
        om = acc_ref[...] / l_ref[...]
        for h in range(DIFF_HEADS):
            d = om[8 * h:8 * h + 1] - lam_full * om[8 * h + 1:8 * h + 2]
            o_ref[0, :, DIFF_VDIM * h:DIFF_VDIM * (h + 1)] = _rms(d, subln_ref[...]) * (1.0 - lam_init)


def _diff_dec(l, dq, d_new, pages, page_table, bias_past, b0, lam, subln, lam_init):
    db = dq.shape[0]
    q4 = (dq * SCALE).reshape(db, DIFF_HEADS, 2, HEAD_DIM)
    q = jnp.zeros((db, DIFF_HEADS, 8, 128), F32)
    q = q.at[:, :, 0, :HEAD_DIM].set(q4[:, :, 0]).at[:, :, 1, HEAD_DIM:].set(q4[:, :, 1])
    knew = d_new[:, :512].reshape(db, DIFF_HEADS, 1, 128)
    vnew = d_new[:, 512:].reshape(db, DIFF_HEADS, 1, 128)
    page_specs = [
        pl.BlockSpec((None, None, PAGE_SIZE * 8, 128),
                     lambda b, s, pt, r=r: (l, _page_of(pt, b, PAGES_PER_STEP * s + r, pages.shape[1]), 0, 0))
        for r in range(PAGES_PER_STEP)]
    grid_spec = pltpu.PrefetchScalarGridSpec(
        num_scalar_prefetch=1,
        grid=(db, N_PAGES // PAGES_PER_STEP),
        in_specs=[
            pl.BlockSpec((1, DIFF_HEADS, 8, 128), lambda b, s, pt: (b, 0, 0, 0)),
            pl.BlockSpec((1, DIFF_HEADS, 1, 128), lambda b, s, pt: (b, 0, 0, 0)),
            pl.BlockSpec((1, DIFF_HEADS, 1, 128), lambda b, s, pt: (b, 0, 0, 0)),
            pl.BlockSpec((4, PAGES_PER_STEP, PAGE_SIZE), lambda b, s, pt: (1, s, 0)),
            pl.BlockSpec((4, 1, 128), lambda b, s, pt: (1, 0, 0)),
            pl.BlockSpec((4, HEAD_DIM), lambda b, s, pt: (0, 0)),
            pl.BlockSpec((1, DIFF_VDIM), lambda b, s, pt: (0, 0)),
        ] + page_specs,
        out_specs=pl.BlockSpec((1, 1, 512), lambda b, s, pt: (b, 0, 0)),
        scratch_shapes=[pltpu.VMEM((8 * DIFF_HEADS, 1), F32), pltpu.VMEM((8 * DIFF_HEADS, 1), F32),
                        pltpu.VMEM((8 * DIFF_HEADS, DIFF_VDIM), F32)],
    )
    out = pl.pallas_call(
        functools.partial(_diff_dec_kernel, lam_init=lam_init),
        grid_spec=grid_spec,
        out_shape=jax.ShapeDtypeStruct((db, 1, 512), F32),
        compiler_params=_cparams(("parallel", "arbitrary")),
        name="diff_decode",
    )(page_table, q, knew, vnew, bias_past, b0, lam, subln, *([pages] * PAGES_PER_STEP))
    return out.reshape(db, 512)


def _moba_gate_kernel(pt_ref, qcol_ref, *rest):
    pages = rest[:PAGES_PER_STEP]
    sel_ref, gate_ref = rest[PAGES_PER_STEP:]
    s = pl.program_id(1)
    lane = lax.broadcasted_iota(jnp.int32, (1, 1, 128), 2)

    @pl.when(s == 0)
    def _():
        gate_ref[...] = jnp.zeros(gate_ref.shape, F32)

    qb = qcol_ref[0]
    per_block = MOBA_BLOCK // PAGE_SIZE
    for blk in range(PAGES_PER_STEP // per_block):
        part = jnp.zeros((MOBA_HEADS, 8, PAGE_SIZE), F32)
        for half in range(per_block):
            prod = pages[per_block * blk + half][...] * qb
            part = part + jnp.sum(prod.reshape(MOBA_HEADS, HEAD_DIM // 8, 8, PAGE_SIZE), axis=1)
        tot = jnp.sum(jnp.sum(part, axis=2, keepdims=True), axis=1, keepdims=True) * (1.0 / MOBA_BLOCK)
        j = s * (PAGES_PER_STEP // per_block) + blk
        gate_ref[...] = jnp.where(lane == j, tot, gate_ref[...])

    @pl.when(s == pl.num_programs(1) - 1)
    def _():
        score = jnp.where(lane < OWN_BLOCK, gate_ref[...], NEG)
        rank = _rank_desc(score, OWN_BLOCK)
        lane_f = lane.astype(F32)
        out = jnp.zeros(score.shape, jnp.int32)
        for r in range(MOBA_TOPK):
            idx = jnp.sum(jnp.where(rank == r, lane_f, 0.0), axis=2, keepdims=True).astype(jnp.int32)
            out = jnp.where(lane == r, idx, out)
        sel_ref[0] = out


def _moba_gate(l, mq, pages, page_table):
    db = mq.shape[0]
    qcol = jnp.broadcast_to(mq.reshape(db, MOBA_HEADS, HEAD_DIM, 1), (db, MOBA_HEADS, HEAD_DIM, PAGE_SIZE))
    page_specs = [
        pl.BlockSpec((None, None, None, MOBA_HEADS, HEAD_DIM, PAGE_SIZE),
                     lambda b, s, pt, r=r: (l, _page_of(pt, b, PAGES_PER_STEP * s + r, pages.shape[1]), 0, 0, 0, 0))
        for r in range(PAGES_PER_STEP)]
    grid_spec = pltpu.PrefetchScalarGridSpec(
        num_scalar_prefetch=1,
        grid=(db, N_PAGES // PAGES_PER_STEP),
        in_specs=[pl.BlockSpec((1, MOBA_HEADS, HEAD_DIM, PAGE_SIZE), lambda b, s, pt: (b, 0, 0, 0))] + page_specs,
        out_specs=pl.BlockSpec((1, MOBA_HEADS, 1, 128), lambda b, s, pt: (b, 0, 0, 0)),
        scratch_shapes=[pltpu.VMEM((MOBA_HEADS, 1, 128), F32)],
    )
    sel = pl.pallas_call(
        _moba_gate_kernel,
        grid_spec=grid_spec,
        out_shape=jax.ShapeDtypeStruct((db, MOBA_HEADS, 1, 128), jnp.int32),
        compiler_params=_cparams(("parallel", "arbitrary")),
        name="moba_gate",
    )(page_table, qcol, *([pages] * PAGES_PER_STEP))
    sel = jnp.clip(sel[:, :, 0, :MOBA_TOPK].reshape(db, MOBA_HEADS * MOBA_TOPK), 0, OWN_BLOCK - 1)
    return jnp.pad(sel, ((0, 0), (0, 128 - MOBA_HEADS * MOBA_TOPK)))


def _moba_dec_kernel(pt_ref, sel_ref, q_ref, knew_ref, vnew_ref, bias_ref, b0_ref, *rest):
    per_head = MOBA_TOPK * (MOBA_BLOCK // PAGE_SIZE)
    pages = rest[:MOBA_HEADS * per_head]
    o_ref = rest[MOBA_HEADS * per_head]
    b = pl.program_id(0)
    outs = []
    for h in range(MOBA_HEADS):
        qh = q_ref[0, h]
        qb = qh.astype(BF16)
        s_self = jnp.sum(qh * knew_ref[0, h], axis=-1, keepdims=True) + b0_ref[h][:, 0:1]
        logits = []
        for i in range(per_head):
            page = jnp.clip((MOBA_BLOCK // PAGE_SIZE) * sel_ref[b, MOBA_TOPK * h + i // 2] + i % 2, 0, N_PAGES - 1)
            kt = pages[per_head * h + i][0].astype(BF16)
            logits.append(jnp.dot(qb, kt, preferred_element_type=F32) + bias_ref[h, pl.ds(page, 1), :])
        sc = jnp.concatenate(logits, axis=1)
        m = jnp.maximum(jnp.max(sc, axis=-1, keepdims=True), s_self)
        p = jnp.exp(sc - m)
        p_self = jnp.exp(s_self - m)
        den = jnp.sum(p, axis=-1, keepdims=True) + p_self
        pv = p_self * vnew_ref[0, h]
        for i in range(per_head):
            vt = pages[per_head * h + i][1].astype(BF16)
            pv = pv + _dot_nt(p[:, PAGE_SIZE * i:PAGE_SIZE * (i + 1)].astype(BF16), vt)
        outs.append((pv / den)[0:1])
    o_ref[0] = jnp.concatenate(outs, axis=1)


def _moba_dec(l, mq, m_new, sel, pages, page_table, bias_past, b0):
    db = mq.shape[0]
    q = jnp.zeros((db, MOBA_HEADS, 8, HEAD_DIM), F32).at[:, :, 0].set((mq * SCALE).reshape(db, MOBA_HEADS, HEAD_DIM))
    knew = m_new[:, :256].reshape(db, MOBA_HEADS, 1, HEAD_DIM)
    vnew = m_new[:, 256:].reshape(db, MOBA_HEADS, 1, HEAD_DIM)
    per_block = MOBA_BLOCK // PAGE_SIZE
    n_pool = pages.shape[1]
    page_specs = [
        pl.BlockSpec((None, None, 2, None, HEAD_DIM, PAGE_SIZE),
                     lambda b, pt, sl, h=h, r=r, half=half:
                     (l, _page_of(pt, b, per_block * _picked(sl, b, MOBA_TOPK * h + r) + half, n_pool), 0, h, 0, 0))
        for h in range(MOBA_HEADS) for r in range(MOBA_TOPK) for half in range(per_block)]
    grid_spec = pltpu.PrefetchScalarGridSpec(
        num_scalar_prefetch=2,
        grid=(db,),
        in_specs=[
            pl.BlockSpec((1, MOBA_HEADS, 8, HEAD_DIM), lambda b, pt, sl: (b, 0, 0, 0)),
            pl.BlockSpec((1, MOBA_HEADS, 1, HEAD_DIM), lambda b, pt, sl: (b, 0, 0, 0)),
            pl.BlockSpec((1, MOBA_HEADS, 1, HEAD_DIM), lambda b, pt, sl: (b, 0, 0, 0)),
            pl.BlockSpec((4, N_PAGES, PAGE_SIZE), lambda b, pt, sl: (0, 0, 0)),
            pl.BlockSpec((4, 1, 128), lambda b, pt, sl: (0, 0, 0)),
        ] + page_specs,
        out_specs=pl.BlockSpec((1, 1, 256), lambda b, pt, sl: (b, 0, 0)),
    )
    out = pl.pallas_call(
        _moba_dec_kernel,
        grid_spec=grid_spec,
        out_shape=jax.ShapeDtypeStruct((db, 1, 256), F32),
        compiler_params=_cparams(("parallel",)),
        name="moba_decode",
    )(page_table, sel, q, knew, vnew, bias_past, b0, *([pages] * len(page_specs)))
    return out.reshape(db, 256)


def _nsa_cmp_dec_kernel(pt_ref, q_ref, w1_ref, pos_ref, w1raw_ref, w2_ref, gkc_ref, ovl_ref, *rest):
    pages = rest[:PAGES_PER_STEP]
    ocmp_ref, sel_ref, xt_ref, cst_ref, carry_ref, m_ref, l_ref, acc_ref, imp_ref = rest[PAGES_PER_STEP:]
    b = pl.program_id(0)
    s = pl.program_id(1)
    rows = PAGES_PER_STEP * PAGE_SIZE // NSA_CMP_STRIDE

    @pl.when((b == 0) & (s == 0))
    def _():
        for kv in range(2):
            pos = jnp.broadcast_to(pos_ref[kv], (8, pos_ref.shape[-1]))
            cst_ref[:, 128 * kv:128 * (kv + 1)] = jnp.dot(
                pos, w1raw_ref[kv], preferred_element_type=F32, precision=HI)[0:1]

    @pl.when(s == 0)
    def _():
        _init_state(m_ref, l_ref, acc_ref)
        imp_ref[...] = jnp.zeros(imp_ref.shape, F32)
        carry_ref[...] = jnp.zeros(carry_ref.shape, F32)

    for r in range(PAGES_PER_STEP):
        xt_ref[PAGE_SIZE * r:PAGE_SIZE * (r + 1), :] = pages[r][...].reshape(2 * HEAD_DIM, PAGE_SIZE).T
    a = jnp.zeros((rows, 512), F32)
    for rr in range(NSA_CMP_STRIDE):
        lhs = xt_ref[pl.ds(rr, rows, stride=NSA_CMP_STRIDE), :].astype(BF16)
        a = a + jnp.dot(lhs, w1_ref[rr], preferred_element_type=F32)
    first = lax.broadcasted_iota(jnp.int32, (rows, 1), 0) == 0
    outs = []
    for kv in range(2):
        top = a[:, 256 * kv:256 * kv + 128]
        bot = a[:, 256 * kv + 128:256 * kv + 256]
        top_prev = jnp.where(first, carry_ref[:, 128 * kv:128 * (kv + 1)], pltpu.roll(top, 1, 0))
        hid = top_prev + bot + cst_ref[:, 128 * kv:128 * (kv + 1)]
        outs.append(jnp.dot(jax.nn.gelu(hid), w2_ref[kv], preferred_element_type=F32, precision=HI))
        carry_ref[:, 128 * kv:128 * (kv + 1)] = top[rows - 1:rows]
    kc = _rms(outs[0], gkc_ref[...])
    vc = outs[1].astype(BF16)

    q = q_ref[0] * SCALE
    col = lax.broadcasted_iota(jnp.int32, (1, rows), 1)
    sc = jnp.where(col + s > 0, _dot_nt(q, kc, precision=HI), NEG)
    m_old = m_ref[0]
    m_new = jnp.maximum(m_old, jnp.max(sc, axis=-1, keepdims=True))
    alpha = jnp.exp(m_old - m_new)
    p = jnp.exp(sc - m_new)
    l_ref[0] = alpha * l_ref[0] + jnp.sum(p, axis=-1, keepdims=True)
    acc_ref[0] = alpha * acc_ref[0] + jnp.dot(p.astype(BF16), vc, preferred_element_type=F32)
    imp_ref[...] = alpha * imp_ref[...] + jnp.dot(p, ovl_ref[...], preferred_element_type=F32, precision=HI)
    m_ref[0] = m_new

    @pl.when(s == pl.num_programs(1) - 1)
    def _():
        ocmp_ref[0] = _row_of_heads(acc_ref[0] / l_ref[0], NSA_HEADS)
        impn = imp_ref[...] / l_ref[0]
        imp = impn[0:1] + impn[1:2] + impn[2:3] + impn[3:4]
        lane = lax.broadcasted_iota(jnp.int32, (1, IMP_W), 1)
        forced = (lane == 0) | (lane == CUR_SEL) | (lane == CUR_SEL - 1)
        score = jnp.where(lane <= CUR_SEL, imp + jnp.where(forced, FORCE_BONUS, 0.0), NEG)
        rank = _rank_desc(score, CUR_SEL + 1)
        lane_f = lane.astype(F32)
        lane_o = lax.broadcasted_iota(jnp.int32, (1, 128), 1)
        out = jnp.zeros((1, 128), jnp.int32)
        for r in range(NSA_SEL_TOPK):
            idx = jnp.sum(jnp.where(rank == r, lane_f, 0.0), axis=1, keepdims=True).astype(jnp.int32)
            out = jnp.where(lane_o == r, idx, out)
        sel_ref[0] = out


def _nsa_cmp_dec(l, nq, pages, page_table, w1dec, pos_flat, w1raw, w2, gkc, ovl_dec):
    db = nq.shape[0]
    q = jnp.zeros((db, 8, HEAD_DIM), F32).at[:, :NSA_HEADS].set(nq.reshape(db, NSA_HEADS, HEAD_DIM))
    rows = PAGES_PER_STEP * PAGE_SIZE // NSA_CMP_STRIDE
    page_specs = [
        pl.BlockSpec((None, None, 2, HEAD_DIM, PAGE_SIZE),
                     lambda b, s, pt, r=r: (l, _page_of(pt, b, PAGES_PER_STEP * s + r, pages.shape[1]), 0, 0, 0))
        for r in range(PAGES_PER_STEP)]
    full = lambda a: pl.BlockSpec(a.shape, lambda b, s, pt: (0,) * a.ndim)
    grid_spec = pltpu.PrefetchScalarGridSpec(
        num_scalar_prefetch=1,
        grid=(db, N_PAGES // PAGES_PER_STEP),
        in_specs=[pl.BlockSpec((1, 8, HEAD_DIM), lambda b, s, pt: (b, 0, 0)),
                  full(w1dec), full(pos_flat), full(w1raw), full(w2), full(gkc),
                  pl.BlockSpec((rows, IMP_W), lambda b, s, pt: (s, 0))] + page_specs,
        out_specs=[pl.BlockSpec((1, 1, 256), lambda b, s, pt: (b, 0, 0)),
                   pl.BlockSpec((1, 1, 128), lambda b, s, pt: (b, 0, 0))],
        scratch_shapes=[
            pltpu.VMEM((PAGES_PER_STEP * PAGE_SIZE, 128), F32),
            pltpu.VMEM((1, 256), F32),
            pltpu.VMEM((1, 256), F32),
            pltpu.VMEM((1, 8, 1), F32),
            pltpu.VMEM((1, 8, 1), F32),
            pltpu.VMEM((1, 8, HEAD_DIM), F32),
            pltpu.VMEM((8, IMP_W), F32),
        ],
    )
    o_cmp, sel = pl.pallas_call(
        _nsa_cmp_dec_kernel,
        grid_spec=grid_spec,
        out_shape=[jax.ShapeDtypeStruct((db, 1, 256), F32), jax.ShapeDtypeStruct((db, 1, 128), jnp.int32)],
        compiler_params=_cparams(("arbitrary", "arbitrary")),
        name="nsa_cmp_decode",
    )(page_table, q, w1dec, pos_flat, w1raw, w2, gkc, ovl_dec, *([pages] * PAGES_PER_STEP))
    return o_cmp.reshape(db, 256), jnp.clip(sel[:, 0, :], 0, CUR_SEL)


def _nsa_dec_kernel(pt_ref, sel_ref, q_ref, snew_ref, wnew_ref, wcol_ref, bias_ref, bwin_ref, b0_ref, win_ref,
                    *rest):
    pages = rest[:NSA_SEL_TOPK]
    osel_ref, owin_ref, buf_ref = rest[NSA_SEL_TOPK:]
    b = pl.program_id(0)
    q = q_ref[0] * SCALE
    qb = q.astype(BF16)
    b0 = b0_ref[:, 0:1]

    half_of = lax.broadcasted_iota(jnp.int32, (1, PAGE_SIZE), 1) // NSA_SEL_BLOCK
    logits = []
    for r in range(NSA_SEL_TOPK):
        j = sel_ref[b, r]
        page = jnp.clip(j // 2, 0, N_PAGES - 1)
        want = jnp.where(j < CUR_SEL, j % 2, 2)
        sc = jnp.dot(qb, pages[r][0].astype(BF16), preferred_element_type=F32) + bias_ref[page]
        logits.append(jnp.where(half_of == want, sc, NEG))
    sc = jnp.concatenate(logits, axis=1)
    s_self = jnp.sum(q * snew_ref[0, 0:1], axis=-1, keepdims=True) + b0
    m = jnp.maximum(jnp.max(sc, axis=-1, keepdims=True), s_self)
    p = jnp.exp(sc - m)
    p_self = jnp.exp(s_self - m)
    den = jnp.sum(p, axis=-1, keepdims=True) + p_self
    pv = p_self * snew_ref[0, 1:2]
    for r in range(NSA_SEL_TOPK):
        pv = pv + _dot_nt(p[:, PAGE_SIZE * r:PAGE_SIZE * (r + 1)].astype(BF16), pages[r][1].astype(BF16))
    osel_ref[0] = _row_of_heads(pv / den, NSA_HEADS)

    wlen = win_ref.shape[-1]
    lane = lax.broadcasted_iota(jnp.int32, (1, wlen), 1)
    sw = jnp.dot(qb, win_ref[0].astype(BF16), preferred_element_type=F32) + bwin_ref[...]
    sw = jnp.where(lane == 0, NEG, sw)
    s_self = jnp.sum(q * wnew_ref[0, 0:1], axis=-1, keepdims=True) + b0
    m = jnp.maximum(jnp.max(sw, axis=-1, keepdims=True), s_self)
    p = jnp.exp(sw - m)
    p_self = jnp.exp(s_self - m)
    den = jnp.sum(p, axis=-1, keepdims=True) + p_self
    pv = p_self * wnew_ref[0, 1:2] + _dot_nt(p.astype(BF16), win_ref[1].astype(BF16))
    owin_ref[0] = _row_of_heads(pv / den, NSA_HEADS)
    last = lax.broadcasted_iota(jnp.int32, (1, 128), 1) == 127
    for kv in range(2):
        shifted = pltpu.roll(win_ref[kv], wlen - 1, 1)
        buf_ref[0, kv, :, 0:wlen - 128] = shifted[:, 0:wlen - 128]
        buf_ref[0, kv, :, wlen - 128:wlen] = jnp.where(last, wcol_ref[0, kv], shifted[:, wlen - 128:wlen])


def _nsa_dec(l, nq, n_new, w_new, sel, pages, win, page_table, bias_nsa, bwin, b0n):
    db = nq.shape[0]
    wlen = win.shape[-1]
    q = jnp.zeros((db, 8, HEAD_DIM), F32).at[:, :NSA_HEADS].set(nq.reshape(db, NSA_HEADS, HEAD_DIM))
    snew = n_new[:, 128:].reshape(db, 2, HEAD_DIM)
    wnew = w_new.reshape(db, 2, HEAD_DIM)
    wcol = jnp.broadcast_to(w_new.reshape(db, 2, HEAD_DIM, 1), (db, 2, HEAD_DIM, 128))
    n_pool = pages.shape[1]
    page_specs = [
        pl.BlockSpec((None, None, 2, HEAD_DIM, PAGE_SIZE),
                     lambda b, pt, sl, r=r: (l, _page_of(pt, b, _picked(sl, b, r) // 2, n_pool), 1, 0, 0))
        for r in range(NSA_SEL_TOPK)]
    full = lambda a: pl.BlockSpec(a.shape, lambda b, pt, sl: (0,) * a.ndim)
    grid_spec = pltpu.PrefetchScalarGridSpec(
        num_scalar_prefetch=2,
        grid=(db,),
        in_specs=[pl.BlockSpec((1, 8, HEAD_DIM), lambda b, pt, sl: (b, 0, 0)),
                  pl.BlockSpec((1, 2, HEAD_DIM), lambda b, pt, sl: (b, 0, 0)),
                  pl.BlockSpec((1, 2, HEAD_DIM), lambda b, pt, sl: (b, 0, 0)),
                  pl.BlockSpec((1, 2, HEAD_DIM, 128), lambda b, pt, sl: (b, 0, 0, 0)),
                  full(bias_nsa), full(bwin), full(b0n),
                  pl.BlockSpec((None, None, 2, HEAD_DIM, wlen), lambda b, pt, sl: (l, b, 0, 0, 0))] + page_specs,
        out_specs=[pl.BlockSpec((1, 1, 256), lambda b, pt, sl: (b, 0, 0)),
                   pl.BlockSpec((1, 1, 256), lambda b, pt, sl: (b, 0, 0)),
                   pl.BlockSpec((1, 2, HEAD_DIM, wlen), lambda b, pt, sl: (b, 0, 0, 0))],
    )
    o_sel, o_win, buf = pl.pallas_call(
        _nsa_dec_kernel,
        grid_spec=grid_spec,
        out_shape=[jax.ShapeDtypeStruct((db, 1, 256), F32), jax.ShapeDtypeStruct((db, 1, 256), F32),
                   jax.ShapeDtypeStruct((db, 2, HEAD_DIM, wlen), F32)],
        compiler_params=_cparams(("parallel",)),
        name="nsa_decode",
    )(page_table, sel, q, snew, wnew, wcol, bias_nsa, bwin, b0n, win, *([pages] * NSA_SEL_TOPK))
    return o_sel.reshape(db, 256), o_win.reshape(db, 256), jnp.transpose(buf, (0, 3, 1, 2))


def _decode_bias(rel_bias, wlen):
    flat = rel_bias.T[:, _rel_bucket(PAST_LEN - jnp.arange(PAST_LEN))]
    past = flat.reshape(-1, N_PAGES, PAGE_SIZE)
    first_nsa = MOBA_HEADS + DIFF_HEADS
    bias_nsa = jnp.pad(jnp.transpose(past[first_nsa:], (1, 0, 2)), ((0, 0), (0, 8 - NSA_HEADS), (0, 0)))
    bwin = jnp.pad(flat[first_nsa:, PAST_LEN - wlen:], ((0, 8 - NSA_HEADS), (0, 0)))
    zero = rel_bias[_rel_bucket(jnp.zeros((), jnp.int32))]
    b0 = jnp.broadcast_to(zero[:, None, None], (zero.shape[0], 1, 128))
    b0n = jnp.pad(jnp.broadcast_to(zero[first_nsa:, None], (NSA_HEADS, 128)), ((0, 8 - NSA_HEADS), (0, 0)))
    return past, bias_nsa, bwin, b0, b0n


def _decode_overlap():
    n = np.arange(N_PAGES * PAGE_SIZE // NSA_CMP_STRIDE) - 1
    lo = np.arange(IMP_W) * NSA_SEL_BLOCK
    start = n[:, None] * NSA_CMP_STRIDE
    ovl = (start < lo[None, :] + NSA_SEL_BLOCK) & (start + NSA_CMP_BLOCK > lo[None, :]) & (n[:, None] >= 0)
    return jnp.asarray(ovl, F32)


def _pack_layer(l, norm_gain, ffn_w1, ffn_w2, w_in, qk_gain, cmp_pos, cmp_w1, cmp_w2,
                w_branch_moba, w_branch_diff, w_branch_nsa, w_out):
    w = w_in[l]
    w_pack = jnp.concatenate(
        [w[:, :N_MAIN], jnp.pad(w[:, N_MAIN:N_MAIN + 12], ((0, 0), (0, 116)))], axis=1).astype(BF16)
    g = qk_gain[l]
    ones = jnp.ones((HEAD_DIM,), F32)
    zeros = jnp.zeros((HEAD_DIM,), F32)

    def rep(v, k):
        return jnp.tile(v, k)

    gain_row = jnp.concatenate([
        rep(g[0], 4), rep(g[1], 4), rep(ones, 4), rep(g[2], 8), rep(g[3], 8), rep(ones, 8), rep(g[4], 4),
        ones, ones, g[6], ones, g[7], ones])[None, :]
    nmask_row = jnp.concatenate([
        rep(ones, 8), rep(zeros, 4), rep(ones, 16), rep(zeros, 8), rep(ones, 4),
        zeros, zeros, ones, zeros, ones, zeros])[None, :]
    w1 = cmp_w1[l].reshape(2, 2, NSA_CMP_STRIDE, HEAD_DIM, NSA_CMP_HIDDEN)
    zero = jnp.zeros((NSA_CMP_STRIDE, HEAD_DIM, NSA_CMP_HIDDEN), F32)
    k_cols = jnp.concatenate([w1[0, 0], w1[0, 1], zero, zero], axis=-1)
    v_cols = jnp.concatenate([zero, zero, w1[1, 0], w1[1, 1]], axis=-1)
    w1cat = jnp.concatenate([k_cols, v_cols], axis=1).reshape(NSA_CMP_STRIDE * 128, 512)
    return dict(
        g0=norm_gain[l, 0][None, :], g1=norm_gain[l, 1][None, :], g2=norm_gain[l, 2][None, :],
        ffn1=(ffn_w1[l, 0].astype(BF16), ffn_w2[l, 0].astype(BF16)),
        ffn2=(ffn_w1[l, 1].astype(BF16), ffn_w2[l, 1].astype(BF16)),
        w_pack=w_pack, gain_row=gain_row, nmask_row=nmask_row,
        wbg=w[:, N_MAIN + 12:].astype(BF16),
        w1cat=w1cat, pos_flat=cmp_pos[l].reshape(2, 1, NSA_CMP_BLOCK * HEAD_DIM), w1raw=cmp_w1[l],
        w2x4=jnp.tile(cmp_w2[l], (1, 1, 4)), gkc4=jnp.tile(g[5], 4)[None, :],
        wa=w_branch_moba[l].astype(BF16), wb=w_branch_diff[l].astype(BF16), wc=w_branch_nsa[l].astype(BF16),
        wout=w_out[l].astype(BF16),
    )


def _bias_tiles(rel_bias):
    qi = jnp.arange(TQ)[:, None]
    ki = jnp.arange(TQ)[None, :]
    dist = jnp.stack([jnp.maximum(qi - ki, 0), TQ + qi - ki, 2 * TQ + qi - ki])
    onehot = (_rel_bucket(dist)[..., None] == jnp.arange(N_BUCKETS)).astype(F32)
    tiles = jnp.einsum('tqkn,nh->htqk', onehot, rel_bias.astype(F32), precision=HI)
    causal = jnp.stack([qi >= ki, jnp.ones((TQ, TQ), bool), jnp.ones((TQ, TQ), bool)])
    return jnp.where(causal, tiles, NEG)


def _const_tables(seq):
    blocks = np.arange(seq // NSA_CMP_STRIDE) * NSA_CMP_STRIDE
    lo = np.arange(seq // NSA_SEL_BLOCK) * NSA_SEL_BLOCK
    overlap = ((blocks[:, None] < lo[None, :] + NSA_SEL_BLOCK) & (blocks[:, None] + NSA_CMP_BLOCK > lo[None, :]))
    expand = np.arange(seq // NSA_SEL_BLOCK)[:, None] == (np.arange(seq)[None, :] // NSA_SEL_BLOCK)
    gexp = np.zeros((128, 768), np.float32)
    for k in range(3):
        for h in range(NSA_HEADS):
            gexp[k * NSA_HEADS + h, 256 * k + HEAD_DIM * h:256 * k + HEAD_DIM * (h + 1)] = 1.0
    bd = np.kron(np.eye(4, dtype=np.float32), np.ones((HEAD_DIM, HEAD_DIM), np.float32))
    return (jnp.asarray(overlap, F32), jnp.asarray(expand, BF16), jnp.asarray(gexp), jnp.asarray(bd, BF16))


def kernel(x_prompt, x_sample, cache_moba_kv, cache_diff_kv, cache_nsa_kv, state_nsa_win, page_table, rel_bias, norm_gain, ffn_w1, ffn_w2, w_in, qk_gain, diff_lambda, diff_subln, cmp_pos, cmp_w1, cmp_w2, w_branch_moba, w_branch_diff, w_branch_nsa, w_out):
    bsz, seq, _ = x_prompt.shape
    db, dq_len, _ = x_sample.shape
    n_p = bsz * seq
    n_s = db * dq_len
    assert dq_len == 1 and OWN_BLOCK >= MOBA_TOPK and CUR_SEL + 1 >= NSA_SEL_TOPK
    bias_tiles = _bias_tiles(rel_bias)
    overlap, expand, gexp, bd = _const_tables(seq)
    n_pool = cache_diff_kv.shape[1]
    moba_pages = jnp.transpose(cache_moba_kv, (0, 1, 3, 4, 5, 2))
    diff_pages = cache_diff_kv.reshape(DEPTH, n_pool, PAGE_SIZE * 8, 128)
    nsa_pages = jnp.transpose(cache_nsa_kv, (0, 1, 3, 4, 2))
    win_state = jnp.transpose(state_nsa_win, (0, 1, 3, 4, 2))
    bias_past, bias_nsa, bwin, b0, b0n = _decode_bias(rel_bias, win_state.shape[-1])
    ovl_dec = _decode_overlap()

    xp = x_prompt.reshape(n_p, D_MODEL)
    xs = x_sample.reshape(n_s, D_MODEL)
    outs = [[] for _ in range(8)]
    for l in range(DEPTH):
        lam_init = 0.8 - 0.6 * math.exp(-0.3 * l)
        pk = _pack_layer(l, norm_gain, ffn_w1, ffn_w2, w_in, qk_gain, cmp_pos, cmp_w1, cmp_w2,
                         w_branch_moba, w_branch_diff, w_branch_nsa, w_out)
        xp = _ffn(xp, pk['g0'], *pk['ffn1'], tm=1024)
        xs = _ffn(xs, pk['g0'], *pk['ffn1'], tm=n_s)

        mq, m_row, dq, d_row, nq, n_row, w_row, ngate = _inproj(
            xp, pk['g1'], pk['w_pack'], pk['gain_row'], pk['nmask_row'], bd, tm=256)
        r3 = lambda a: a.reshape(bsz, seq, a.shape[-1])
        mq, m_row, dq, d_row, nq, n_row, w_row = map(r3, (mq, m_row, dq, d_row, nq, n_row, w_row))
        o_a = _moba_prompt(mq, m_row, _moba_kmean(m_row), bias_tiles)
        o_b = _diff_prompt(dq, d_row, bias_tiles, diff_lambda[l], diff_subln[l][None, :], lam_init)
        kc4, vc4 = _nsa_compress(n_row[:, :, :128], pk['w1cat'], pk['pos_flat'], pk['w1raw'], pk['w2x4'], pk['gkc4'])
        o_cmp, selmask = _nsa_cmp_select(nq, kc4, vc4, overlap)
        o_sel = _nsa_selected(nq, n_row, selmask, expand, bias_tiles)
        o_win = _nsa_window(nq, w_row, bias_tiles)
        f2 = lambda a: a.reshape(n_p, a.shape[-1])
        xp = _merge(xp, pk['g1'], f2(o_a), f2(o_b), f2(o_cmp), f2(o_sel), f2(o_win), ngate,
                    pk['wbg'], pk['wa'], pk['wb'], pk['wc'], pk['wout'], gexp, tm=256)
        outs[0].append(m_row.reshape(bsz, seq, 2, MOBA_HEADS, HEAD_DIM))
        outs[2].append(d_row.reshape(bsz, seq, 2, DIFF_HEADS, DIFF_VDIM))
        outs[4].append(n_row.reshape(bsz, seq, 4, HEAD_DIM))
        outs[6].append(w_row[:, seq - min(NSA_WINDOW, seq):].reshape(bsz, min(NSA_WINDOW, seq), 2, HEAD_DIM))

        mq, m_row, dq, d_row, nq, n_row, w_row, ngate = _inproj(
            xs, pk['g1'], pk['w_pack'], pk['gain_row'], pk['nmask_row'], bd, tm=n_s)
        o_a = _moba_dec(l, mq, m_row, _moba_gate(l, mq, moba_pages, page_table), moba_pages, page_table,
                        bias_past, b0)
        o_b = _diff_dec(l, dq, d_row, diff_pages, page_table, bias_past, b0, diff_lambda[l],
                        diff_subln[l][None, :], lam_init)
        o_cmp, nsa_sel = _nsa_cmp_dec(l, nq, nsa_pages, page_table, pk['w1cat'].reshape(NSA_CMP_STRIDE, 128, 512)
                                      .astype(BF16), pk['pos_flat'], pk['w1raw'], cmp_w2[l], qk_gain[l, 5][None, :],
                                      ovl_dec)
        o_sel, o_win, buf = _nsa_dec(l, nq, n_row, w_row, nsa_sel, nsa_pages, win_state, page_table,
                                     bias_nsa, bwin, b0n)
        xs = _merge(xs, pk['g1'], o_a, o_b, o_cmp, o_sel, o_win, ngate,
                    pk['wbg'], pk['wa'], pk['wb'], pk['wc'], pk['wout'], gexp, tm=n_s)
        outs[1].append(m_row.reshape(db, dq_len, 2, MOBA_HEADS, HEAD_DIM))
        outs[3].append(d_row.reshape(db, dq_len, 2, DIFF_HEADS, DIFF_VDIM))
        outs[5].append(n_row.reshape(db, dq_len, 4, HEAD_DIM))
        outs[7].append(buf)

        xp = _ffn(xp, pk['g2'], *pk['ffn2'], tm=1024)
        xs = _ffn(xs, pk['g2'], *pk['ffn2'], tm=n_s)
    return (xp.reshape(bsz, seq, D_MODEL), xs.reshape(db, dq_len, D_MODEL)) + tuple(jnp.stack(o) for o in outs)
```

```python
import functools
import math

import numpy as np
import jax
import jax.numpy as jnp
from jax import lax
from jax.experimental import pallas as pl
from jax.experimental.pallas import tpu as pltpu

F32 = jnp.float32
BF16 = jnp.bfloat16
HI = lax.Precision.HIGHEST

D_MODEL = 1024
DEPTH = 2
PAST_LEN = 16384
PAGE_SIZE = 128
HEAD_DIM = 64
MOBA_HEADS = 4
MOBA_BLOCK = 256
MOBA_TOPK = 3
DIFF_HEADS = 4
DIFF_VDIM = 2 * HEAD_DIM
NSA_HEADS = 4
NSA_CMP_BLOCK = 32
NSA_CMP_STRIDE = 16
NSA_CMP_HIDDEN = 128
NSA_SEL_BLOCK = 64
NSA_SEL_TOPK = 16
NSA_WINDOW = 512
N_BUCKETS = 32
MAX_DISTANCE = 128
D_FF = 2688
EPS = 1e-6
NEG = -1e30
FORCE_BONUS = 1e4
SCALE = HEAD_DIM ** -0.5

TQ = 256
N_MAIN = 2944
N_PACK = 3072
VMEM_LIMIT = 56 * 1024 * 1024


def _cparams(sem):
    return pltpu.CompilerParams(dimension_semantics=sem, vmem_limit_bytes=VMEM_LIMIT)


def _rms(x, g):
    return x * lax.rsqrt(jnp.mean(x * x, axis=-1, keepdims=True) + EPS) * g


def _dot_nt(a, b, precision=None):
    return lax.dot_general(a, b, (((1,), (1,)), ((), ())), preferred_element_type=F32, precision=precision)


def _ffn_kernel(x_ref, g_ref, w1a_ref, w1b_ref, w2_ref, o_ref, u_scr, acc_scr):
    f = pl.program_id(1)

    @pl.when(f == 0)
    def _():
        u_scr[...] = _rms(x_ref[...], g_ref[...]).astype(BF16)
        acc_scr[...] = jnp.zeros_like(acc_scr)

    u = u_scr[...]
    a = jnp.dot(u, w1a_ref[...], preferred_element_type=F32)
    b = jnp.dot(u, w1b_ref[...], preferred_element_type=F32)
    h = (a * jax.nn.sigmoid(a) * b).astype(BF16)
    acc_scr[...] += jnp.dot(h, w2_ref[...], preferred_element_type=F32)

    @pl.when(f == pl.num_programs(1) - 1)
    def _():
        o_ref[...] = x_ref[...] + 0.5 * acc_scr[...]


def _ffn(x, g, w1, w2, tm):
    n = x.shape[0]
    tf = 896
    nf = D_FF // tf
    return pl.pallas_call(
        _ffn_kernel,
        grid=(n // tm, nf),
        in_specs=[
            pl.BlockSpec((tm, D_MODEL), lambda i, f: (i, 0)),
            pl.BlockSpec((1, D_MODEL), lambda i, f: (0, 0)),
            pl.BlockSpec((D_MODEL, tf), lambda i, f: (0, f)),
            pl.BlockSpec((D_MODEL, tf), lambda i, f: (0, f + nf)),
            pl.BlockSpec((tf, D_MODEL), lambda i, f: (f, 0)),
        ],
        out_specs=pl.BlockSpec((tm, D_MODEL), lambda i, f: (i, 0)),
        out_shape=jax.ShapeDtypeStruct((n, D_MODEL), F32),
        scratch_shapes=[pltpu.VMEM((tm, D_MODEL), BF16), pltpu.VMEM((tm, D_MODEL), F32)],
        compiler_params=_cparams(("parallel", "arbitrary")),
        name="ffn",
    )(x, g, w1, w1, w2)


_SLABS = (
    (0, 256, 0, 0),
    (256, 256, 1, 0),
    (512, 256, 1, 256),
    (768, 256, 2, 0),
    (1024, 256, 2, 256),
    (1280, 256, 3, 0),
    (1536, 256, 3, 256),
    (1792, 256, 3, 512),
    (2048, 256, 3, 768),
    (2304, 256, 4, 0),
    (2560, 256, 5, 0),
    (2816, 128, 6, 0),
    (2944, 128, 7, 0),
)


def _inproj_kernel(x_ref, g_ref, w_ref, gain_ref, nmask_ref, bd_ref,
                   mq_ref, mrow_ref, dq_ref, drow_ref, nq_ref, nrow_ref, wrow_ref, ngate_ref):
    outs = (mq_ref, mrow_ref, dq_ref, drow_ref, nq_ref, nrow_ref, wrow_ref, ngate_ref)
    u = _rms(x_ref[...], g_ref[...]).astype(BF16)
    for c0, w, oi, oc in _SLABS:
        z = jnp.dot(u, w_ref[:, c0:c0 + w], preferred_element_type=F32)
        if c0 < N_MAIN and c0 not in (512, 1792, 2048):
            ss = jnp.dot((z * z).astype(BF16), bd_ref[0:w, 0:w], preferred_element_type=F32)
            zn = z * lax.rsqrt(ss * (1.0 / HEAD_DIM) + EPS) * gain_ref[:, c0:c0 + w]
            z = jnp.where(nmask_ref[:, c0:c0 + w] > 0.5, zn, z)
        outs[oi][:, oc:oc + w] = z


def _inproj(x, g, w_pack, gain_row, nmask_row, bd, tm):
    n = x.shape[0]
    widths = (256, 512, 512, 1024, 256, 256, 128, 128)
    const = lambda i: (0, 0)
    return pl.pallas_call(
        _inproj_kernel,
        grid=(n // tm,),
        in_specs=[
            pl.BlockSpec((tm, D_MODEL), lambda i: (i, 0)),
            pl.BlockSpec((1, D_MODEL), const),
            pl.BlockSpec((D_MODEL, N_PACK), const),
            pl.BlockSpec((1, N_MAIN), const),
            pl.BlockSpec((1, N_MAIN), const),
            pl.BlockSpec((256, 256), const),
        ],
        out_specs=[pl.BlockSpec((tm, w), lambda i: (i, 0)) for w in widths],
        out_shape=[jax.ShapeDtypeStruct((n, w), F32) for w in widths],
        compiler_params=_cparams(("parallel",)),
        name="inproj",
    )(x, g, w_pack, gain_row, nmask_row, bd)


def _tri_tables(nq, depth=None):
    ti, tj = [], []
    for i in range(nq):
        lo = 0 if depth is None else max(0, i - depth + 1)
        for j in range(i, lo - 1, -1):
            ti.append(i)
            tj.append(j)
    return jnp.asarray(ti, jnp.int32), jnp.asarray(tj, jnp.int32)


def _softmax_update(s, m_ref, l_ref, narrow=True):
    m_old = m_ref[...]
    m_new = jnp.maximum(m_old, jnp.max(s, axis=-1, keepdims=True))
    alpha = jnp.exp(m_old - m_new)
    p = jnp.exp(s - m_new)
    l_ref[...] = alpha * l_ref[...] + jnp.sum(p, axis=-1, keepdims=True)
    m_ref[...] = m_new
    return alpha, (p.astype(BF16) if narrow else p)


def _softmax_lanes(s, m_ref):
    m_old = m_ref[...]
    m_new = jnp.maximum(m_old, jnp.max(s, axis=-1, keepdims=True))
    alpha = jnp.exp(m_old - m_new)
    p = jnp.exp(s - _lanes(m_new, s.shape[1]))
    m_ref[...] = m_new
    return alpha, p.astype(BF16)


def _lanes(x, width):
    return x if width == 128 else jnp.concatenate([x] * (width // 128), axis=1)


def _init_max_acc(m_ref, acc_ref):
    m_ref[...] = jnp.full(m_ref.shape, NEG, F32)
    acc_ref[...] = jnp.zeros(acc_ref.shape, F32)


def _rows(x, c, n=1):
    return x[TQ * c:TQ * (c + n)]


def _init_state(m_ref, l_ref, acc_ref):
    m_ref[...] = jnp.full(m_ref.shape, NEG, F32)
    l_ref[...] = jnp.zeros(l_ref.shape, F32)
    acc_ref[...] = jnp.zeros(acc_ref.shape, F32)


def _diff_kernel(ti_ref, tj_ref, q_ref, k_ref, v_ref, bias_ref, lam_ref, subln_ref, o_ref,
                 qs_ref, m_ref, acc_ref, *, lam_init):
    t = pl.program_id(1)
    i = ti_ref[t]
    j = tj_ref[t]

    @pl.when(j == i)
    def _():
        q = q_ref[0] * SCALE
        chunk = lax.broadcasted_iota(jnp.int32, (1, 256), 1) // HEAD_DIM
        for c in range(8):
            half = q[:, 256 * (c // 4):256 * (c // 4) + 256]
            qs_ref[TQ * c:TQ * (c + 1)] = jnp.where(chunk == c % 4, half, 0.0).astype(BF16)
        _init_max_acc(m_ref, acc_ref)

    kt = k_ref[0].astype(BF16)
    vt = v_ref[0].astype(BF16)
    tsel = jnp.minimum(i - j, 2)
    logits = []
    for h in range(DIFF_HEADS):
        bias = bias_ref[h, tsel]
        sh = _dot_nt(_rows(qs_ref, 2 * h, 2), kt[:, 256 * (h // 2):256 * (h // 2) + 256])
        logits += [_rows(sh, 0) + bias, _rows(sh, 1) + bias]
    alpha, p = _softmax_lanes(jnp.concatenate(logits, axis=0), m_ref)
    ones = jnp.ones((TQ, 128), BF16)
    pv = [jnp.dot(_rows(p, 2 * h, 2), jnp.concatenate([vt[:, DIFF_VDIM * h:DIFF_VDIM * (h + 1)], ones], axis=1),
                  preferred_element_type=F32) for h in range(DIFF_HEADS)]
    acc_ref[...] = _lanes(alpha, 256) * acc_ref[...] + jnp.concatenate(pv, axis=0)

    @pl.when(j == 0)
    def _():
        lam = lam_ref[...]
        lam_full = (jnp.exp(jnp.sum(lam[0:1] * lam[1:2], axis=-1, keepdims=True))
                    - jnp.exp(jnp.sum(lam[2:3] * lam[3:4], axis=-1, keepdims=True)) + lam_init)
        om = acc_ref[:, 0:DIFF_VDIM] / acc_ref[:, DIFF_VDIM:2 * DIFF_VDIM]
        for h in range(DIFF_HEADS):
            o = _rms(_rows(om, 2 * h) - lam_full * _rows(om, 2 * h + 1), subln_ref[...]) * (1.0 - lam_init)
            o_ref[0, :, DIFF_VDIM * h:DIFF_VDIM * (h + 1)] = o


def _diff_prompt(dq, d_row, bias_tiles, lam, subln, lam_init):
    b, s, _ = dq.shape
    ti, tj = _tri_tables(s // TQ)
    grid_spec = pltpu.PrefetchScalarGridSpec(
        num_scalar_prefetch=2,
        grid=(b, int(ti.shape[0])),
        in_specs=[
            pl.BlockSpec((1, TQ, 512), lambda b_, t, ti_, tj_: (b_, ti_[t], 0)),
            pl.BlockSpec((1, TQ, 512), lambda b_, t, ti_, tj_: (b_, tj_[t], 0)),
            pl.BlockSpec((1, TQ, 512), lambda b_, t, ti_, tj_: (b_, tj_[t], 1)),
            pl.BlockSpec((4, 3, TQ, TQ), lambda b_, t, ti_, tj_: (1, 0, 0, 0)),
            pl.BlockSpec((4, HEAD_DIM), lambda b_, t, ti_, tj_: (0, 0)),
            pl.BlockSpec((1, DIFF_VDIM), lambda b_, t, ti_, tj_: (0, 0)),
        ],
        out_specs=pl.BlockSpec((1, TQ, 512), lambda b_, t, ti_, tj_: (b_, ti_[t], 0)),
        scratch_shapes=[
            pltpu.VMEM((8 * TQ, 256), BF16),
            pltpu.VMEM((8 * TQ, 128), F32),
            pltpu.VMEM((8 * TQ, 2 * DIFF_VDIM), F32),
        ],
    )
    return pl.pallas_call(
        functools.partial(_diff_kernel, lam_init=lam_init),
        grid_spec=grid_spec,
        out_shape=jax.ShapeDtypeStruct((b, s, 512), F32),
        compiler_params=_cparams(("parallel", "arbitrary")),
        name="diff_prompt",
    )(ti, tj, dq, d_row, d_row, bias_tiles, lam, subln)


def _rank_desc(score, n):
    axis = score.ndim - 1
    col = lax.broadcasted_iota(jnp.int32, (1,) * axis + (score.shape[-1],), axis)
    rank = jnp.zeros(score.shape, F32)
    for jp in range(n):
        sj = score[..., jp:jp + 1]
        tie = jnp.where(jp < col, 1.0, 0.0)
        rank = rank + jnp.where(sj > score, 1.0, jnp.where(sj == score, tie, 0.0))
    return rank


def _kmean_kernel(k_ref, o_ref, *, nb):
    o_ref[0] = jnp.zeros(o_ref.shape[1:], F32)
    o_ref[0, 0:nb] = jnp.mean(k_ref[0].reshape(nb, MOBA_BLOCK, 256), axis=1)


def _moba_kmean(m_row):
    b, s, _ = m_row.shape
    nb = s // MOBA_BLOCK
    return pl.pallas_call(
        functools.partial(_kmean_kernel, nb=nb),
        grid=(b,),
        in_specs=[pl.BlockSpec((1, s, 256), lambda i: (i, 0, 0))],
        out_specs=pl.BlockSpec((1, 128, 256), lambda i: (i, 0, 0)),
        out_shape=jax.ShapeDtypeStruct((b, 128, 256), F32),
        compiler_params=_cparams(("parallel",)),
        name="moba_kmean",
    )(m_row)


def _moba_kernel(ti_ref, tj_ref, q_ref, k_ref, v_ref, kmean_ref, bias_ref, spread_ref, o_ref,
                 qs_ref, sel_ref, m_ref, acc_ref, *, nb):
    t = pl.program_id(1)
    i = ti_ref[t]
    j = tj_ref[t]
    head = lax.broadcasted_iota(jnp.int32, (1, 256), 1) // HEAD_DIM
    col = lax.broadcasted_iota(jnp.int32, (1, 128), 1)

    @pl.when(j == i)
    def _():
        q = q_ref[0]
        qh = jnp.concatenate([jnp.where(head == h, q, 0.0) for h in range(MOBA_HEADS)], axis=0)
        qs_ref[...] = (qh * SCALE).astype(BF16)
        gate = _dot_nt(qh, kmean_ref[0], precision=HI)
        valid = col < i
        score = jnp.where(valid, gate, NEG)
        col_f = col.astype(F32)
        sel = jnp.zeros(score.shape, F32)
        for _ in range(MOBA_TOPK):
            top = jnp.max(score, axis=-1, keepdims=True)
            first = jnp.min(jnp.where(score == top, col_f, 128.0), axis=-1, keepdims=True)
            pick = col_f == first
            sel = jnp.where(pick, 1.0, sel)
            score = jnp.where(pick, 3.0 * NEG, score)
        sel = jnp.where(valid, sel, 0.0)
        wide = jnp.dot(sel.astype(BF16), spread_ref[...], preferred_element_type=F32)
        for jj in range(nb):
            sel_ref[jj] = wide[:, 128 * jj:128 * (jj + 1)]
        _init_max_acc(m_ref, acc_ref)

    kt = k_ref[0].astype(BF16)
    vt = v_ref[0].astype(BF16)
    tsel = jnp.minimum(i - j, 2)
    bias = jnp.concatenate([bias_ref[h, tsel] for h in range(MOBA_HEADS)], axis=0)
    ok = _lanes(sel_ref[j] + jnp.where(j == i, 1.0, 0.0), TQ) > 0.5
    s = jnp.where(ok, _dot_nt(qs_ref[...], kt) + bias, NEG)
    alpha, p = _softmax_lanes(s, m_ref)
    vx = jnp.concatenate([vt, jnp.ones((TQ, 128), BF16)], axis=1)
    acc_ref[...] = _lanes(alpha, 384) * acc_ref[...] + jnp.dot(p, vx, preferred_element_type=F32)

    @pl.when(j == 0)
    def _():
        om = acc_ref[:, 0:256] / _lanes(acc_ref[:, 256:384], 256)
        o = jnp.zeros((TQ, 256), F32)
        for h in range(MOBA_HEADS):
            o = jnp.where(head == h, _rows(om, h), o)
        o_ref[0] = o


def _moba_prompt(mq, m_row, kmean, bias_tiles):
    b, s, _ = mq.shape
    nb = s // MOBA_BLOCK
    ti, tj = _tri_tables(s // TQ)
    spread = jnp.asarray(np.kron(np.eye(128, nb, dtype=np.float32), np.ones((1, 128), np.float32)), BF16)
    grid_spec = pltpu.PrefetchScalarGridSpec(
        num_scalar_prefetch=2,
        grid=(b, int(ti.shape[0])),
        in_specs=[
            pl.BlockSpec((1, TQ, 256), lambda b_, t, ti_, tj_: (b_, ti_[t], 0)),
            pl.BlockSpec((1, TQ, 256), lambda b_, t, ti_, tj_: (b_, tj_[t], 0)),
            pl.BlockSpec((1, TQ, 256), lambda b_, t, ti_, tj_: (b_, tj_[t], 1)),
            pl.BlockSpec((1, 128, 256), lambda b_, t, ti_, tj_: (b_, 0, 0)),
            pl.BlockSpec((4, 3, TQ, TQ), lambda b_, t, ti_, tj_: (0, 0, 0, 0)),
            pl.BlockSpec((128, nb * 128), lambda b_, t, ti_, tj_: (0, 0)),
        ],
        out_specs=pl.BlockSpec((1, TQ, 256), lambda b_, t, ti_, tj_: (b_, ti_[t], 0)),
        scratch_shapes=[
            pltpu.VMEM((4 * TQ, 256), BF16),
            pltpu.VMEM((nb, 4 * TQ, 128), F32),
            pltpu.VMEM((4 * TQ, 128), F32),
            pltpu.VMEM((4 * TQ, 384), F32),
        ],
    )
    return pl.pallas_call(
        functools.partial(_moba_kernel, nb=nb),
        grid_spec=grid_spec,
        out_shape=jax.ShapeDtypeStruct((b, s, 256), F32),
        compiler_params=_cparams(("parallel", "arbitrary")),
        name="moba_prompt",
    )(ti, tj, mq, m_row, m_row, kmean, bias_tiles, spread)


def _compress_kernel(r_ref, w1_ref, pos_ref, w1raw_ref, w2_ref, gkc_ref, kc_ref, vc_ref):
    a = jnp.dot(r_ref[0], w1_ref[...], preferred_element_type=F32, precision=HI)
    n = a.shape[0]
    outs = (kc_ref, vc_ref)
    for kv in range(2):
        pos = jnp.broadcast_to(pos_ref[kv], (8, pos_ref.shape[-1]))
        cst = jnp.dot(pos, w1raw_ref[kv], preferred_element_type=F32, precision=HI)[0:1]
        top = a[:, 256 * kv:256 * kv + 128]
        bot = a[:, 256 * kv + 128:256 * kv + 256]
        hid = top + pltpu.roll(bot, n - 1, 0) + cst
        out = jnp.dot(jax.nn.gelu(hid), w2_ref[kv], preferred_element_type=F32, precision=HI)
        if kv == 0:
            out = out * lax.rsqrt(jnp.mean(out * out, axis=-1, keepdims=True) + EPS) * gkc_ref[...]
        outs[kv][0] = out


def _nsa_compress(ckv, w1cat, pos_flat, w1raw, w2x4, gkc4):
    b, s, _ = ckv.shape
    n = s // NSA_CMP_STRIDE
    r = ckv.reshape(b, n, NSA_CMP_STRIDE * 128)
    return pl.pallas_call(
        _compress_kernel,
        grid=(b,),
        in_specs=[
            pl.BlockSpec((1, n, NSA_CMP_STRIDE * 128), lambda i: (i, 0, 0)),
            pl.BlockSpec((NSA_CMP_STRIDE * 128, 512), lambda i: (0, 0)),
            pl.BlockSpec((2, 1, NSA_CMP_BLOCK * HEAD_DIM), lambda i: (0, 0, 0)),
            pl.BlockSpec((2, NSA_CMP_BLOCK * HEAD_DIM, NSA_CMP_HIDDEN), lambda i: (0, 0, 0)),
            pl.BlockSpec((2, NSA_CMP_HIDDEN, 256), lambda i: (0, 0, 0)),
            pl.BlockSpec((1, 256), lambda i: (0, 0)),
        ],
        out_specs=[pl.BlockSpec((1, n, 256), lambda i: (i, 0, 0))] * 2,
        out_shape=[jax.ShapeDtypeStruct((b, n, 256), F32)] * 2,
        compiler_params=_cparams(("parallel",)),
        name="nsa_compress",
    )(r, w1cat, pos_flat, w1raw, w2x4, gkc4)


def _nsa_cmp_kernel(q_ref, kc_ref, vc_ref, ovl_ref, o_ref, sel_ref):
    i = pl.program_id(1)
    ncmp = kc_ref.shape[1]
    nsel = ovl_ref.shape[1]
    q = q_ref[0]
    head = lax.broadcasted_iota(jnp.int32, (1, 256), 1) // HEAD_DIM
    tpos = i * TQ + lax.broadcasted_iota(jnp.int32, (TQ, 1), 0)
    blk_end = lax.broadcasted_iota(jnp.int32, (1, ncmp), 1) * NSA_CMP_STRIDE + (NSA_CMP_BLOCK - 1)
    valid = blk_end <= tpos
    vc = vc_ref[0].astype(BF16)
    imp = jnp.zeros((TQ, nsel), F32)
    o = jnp.zeros((TQ, 256), F32)
    for h in range(NSA_HEADS):
        qh = jnp.where(head == h, q, 0.0)
        s = jnp.where(valid, _dot_nt(qh, kc_ref[0], precision=HI) * SCALE, NEG)
        p = jnp.where(valid, jnp.exp(s - jnp.max(s, axis=-1, keepdims=True)), 0.0)
        p = p / jnp.maximum(jnp.sum(p, axis=-1, keepdims=True), 1e-30)
        o = jnp.where(head == h, jnp.dot(p.astype(BF16), vc, preferred_element_type=F32), o)
        imp = imp + jnp.dot(p, ovl_ref[...], preferred_element_type=F32, precision=HI)
    o_ref[0] = o
    jcol = lax.broadcasted_iota(jnp.int32, (1, nsel), 1)
    cur = tpos // NSA_SEL_BLOCK
    forced = (jcol == 0) | (jcol == cur) | (jcol == cur - 1)
    score = jnp.where(jcol <= cur, imp + jnp.where(forced, FORCE_BONUS, 0.0), NEG)
    sel = (jcol <= cur) & (_rank_desc(score, nsel) < NSA_SEL_TOPK)
    sel_ref[0] = jnp.where(sel, 1.0, 0.0)


def _nsa_cmp_select(nq, kc4, vc4, overlap):
    b, s, _ = nq.shape
    ncmp = kc4.shape[1]
    nsel = overlap.shape[1]
    return pl.pallas_call(
        _nsa_cmp_kernel,
        grid=(b, s // TQ),
        in_specs=[
            pl.BlockSpec((1, TQ, 256), lambda b_, i: (b_, i, 0)),
            pl.BlockSpec((1, ncmp, 256), lambda b_, i: (b_, 0, 0)),
            pl.BlockSpec((1, ncmp, 256), lambda b_, i: (b_, 0, 0)),
            pl.BlockSpec((ncmp, nsel), lambda b_, i: (0, 0)),
        ],
        out_specs=[pl.BlockSpec((1, TQ, 256), lambda b_, i: (b_, i, 0)),
                   pl.BlockSpec((1, TQ, nsel), lambda b_, i: (b_, i, 0))],
        out_shape=[jax.ShapeDtypeStruct((b, s, 256), F32), jax.ShapeDtypeStruct((b, s, nsel), F32)],
        compiler_params=_cparams(("parallel", "parallel")),
        name="nsa_cmp_select",
    )(nq, kc4, vc4, overlap)


def _load_q128(q_ref, qs_ref):
    q = q_ref[0] * SCALE
    low = lax.broadcasted_iota(jnp.int32, (1, 128), 1) < HEAD_DIM
    for h in range(NSA_HEADS):
        slab = q[:, 128 * (h // 2):128 * (h // 2) + 128]
        if h % 2:
            slab = pltpu.roll(slab, HEAD_DIM, 1)
        qs_ref[TQ * h:TQ * (h + 1)] = jnp.where(low, slab, 0.0).astype(BF16)


def _kv128_step(mask, kv, tsel, bias_ref, qs_ref, m_ref, acc_ref):
    dots = _dot_nt(qs_ref[...], kv)
    s = jnp.concatenate([jnp.where(mask, _rows(dots, h) + bias_ref[h, tsel], NEG) for h in range(NSA_HEADS)], axis=0)
    alpha, p = _softmax_lanes(s, m_ref)
    low = lax.broadcasted_iota(jnp.int32, (1, 128), 1) < HEAD_DIM
    acc_ref[...] = alpha * acc_ref[...] + jnp.dot(p, jnp.where(low, jnp.ones_like(kv), kv),
                                                  preferred_element_type=F32)


def _store_o128(acc_ref, o_ref):
    low = lax.broadcasted_iota(jnp.int32, (1, 128), 1) < HEAD_DIM
    acc = acc_ref[...]
    om = acc / pltpu.roll(acc, HEAD_DIM, 1)
    for pair in range(NSA_HEADS // 2):
        even = pltpu.roll(_rows(om, 2 * pair), HEAD_DIM, 1)
        o_ref[0, :, 128 * pair:128 * pair + 128] = jnp.where(low, even, _rows(om, 2 * pair + 1))


def _nsa_sel_kernel(ti_ref, tj_ref, q_ref, kv_ref, selm_ref, exp_ref, bias_ref, o_ref,
                    qs_ref, m_ref, acc_ref):
    t = pl.program_id(1)
    i = ti_ref[t]
    j = tj_ref[t]

    @pl.when(j == i)
    def _():
        _load_q128(q_ref, qs_ref)
        _init_max_acc(m_ref, acc_ref)

    kv = kv_ref[0].astype(BF16)
    picked = jnp.dot(selm_ref[0].astype(BF16), exp_ref[...], preferred_element_type=F32) > 0.5
    _kv128_step(picked, kv, jnp.minimum(i - j, 2), bias_ref, qs_ref, m_ref, acc_ref)

    @pl.when(j == 0)
    def _():
        _store_o128(acc_ref, o_ref)


def _nsa_win_kernel(ti_ref, tj_ref, q_ref, kv_ref, bias_ref, o_ref, qs_ref, m_ref, acc_ref):
    t = pl.program_id(1)
    i = ti_ref[t]
    j = tj_ref[t]

    @pl.when(j == i)
    def _():
        _load_q128(q_ref, qs_ref)
        _init_max_acc(m_ref, acc_ref)

    kv = kv_ref[0].astype(BF16)
    dist = ((i - j) * TQ + lax.broadcasted_iota(jnp.int32, (TQ, TQ), 0)
            - lax.broadcasted_iota(jnp.int32, (TQ, TQ), 1))
    _kv128_step(dist < NSA_WINDOW, kv, jnp.minimum(i - j, 2), bias_ref, qs_ref, m_ref, acc_ref)

    @pl.when((j == 0) | (i - j == NSA_WINDOW // TQ))
    def _():
        _store_o128(acc_ref, o_ref)


def _kv128_scratch():
    return [
        pltpu.VMEM((4 * TQ, 128), BF16),
        pltpu.VMEM((4 * TQ, 128), F32),
        pltpu.VMEM((4 * TQ, 128), F32),
    ]


def _nsa_selected(nq, n_row, selmask, expand, bias_tiles):
    b, s, _ = nq.shape
    nsel = selmask.shape[-1]
    ti, tj = _tri_tables(s // TQ)
    grid_spec = pltpu.PrefetchScalarGridSpec(
        num_scalar_prefetch=2,
        grid=(b, int(ti.shape[0])),
        in_specs=[
            pl.BlockSpec((1, TQ, 256), lambda b_, t, ti_, tj_: (b_, ti_[t], 0)),
            pl.BlockSpec((1, TQ, 128), lambda b_, t, ti_, tj_: (b_, tj_[t], 1)),
            pl.BlockSpec((1, TQ, nsel), lambda b_, t, ti_, tj_: (b_, ti_[t], 0)),
            pl.BlockSpec((nsel, TQ), lambda b_, t, ti_, tj_: (0, tj_[t])),
            pl.BlockSpec((4, 3, TQ, TQ), lambda b_, t, ti_, tj_: (2, 0, 0, 0)),
        ],
        out_specs=pl.BlockSpec((1, TQ, 256), lambda b_, t, ti_, tj_: (b_, ti_[t], 0)),
        scratch_shapes=_kv128_scratch(),
    )
    return pl.pallas_call(
        _nsa_sel_kernel,
        grid_spec=grid_spec,
        out_shape=jax.ShapeDtypeStruct((b, s, 256), F32),
        compiler_params=_cparams(("parallel", "arbitrary")),
        name="nsa_selected",
    )(ti, tj, nq, n_row, selmask, expand, bias_tiles)


def _nsa_window(nq, w_row, bias_tiles):
    b, s, _ = nq.shape
    ti, tj = _tri_tables(s // TQ, depth=NSA_WINDOW // TQ + 1)
    grid_spec = pltpu.PrefetchScalarGridSpec(
        num_scalar_prefetch=2,
        grid=(b, int(ti.shape[0])),
        in_specs=[
            pl.BlockSpec((1, TQ, 256), lambda b_, t, ti_, tj_: (b_, ti_[t], 0)),
            pl.BlockSpec((1, TQ, 128), lambda b_, t, ti_, tj_: (b_, tj_[t], 0)),
            pl.BlockSpec((4, 3, TQ, TQ), lambda b_, t, ti_, tj_: (2, 0, 0, 0)),
        ],
        out_specs=pl.BlockSpec((1, TQ, 256), lambda b_, t, ti_, tj_: (b_, ti_[t], 0)),
        scratch_shapes=_kv128_scratch(),
    )
    return pl.pallas_call(
        _nsa_win_kernel,
        grid_spec=grid_spec,
        out_shape=jax.ShapeDtypeStruct((b, s, 256), F32),
        compiler_params=_cparams(("parallel", "arbitrary")),
        name="nsa_window",
    )(ti, tj, nq, w_row, bias_tiles)


def _merge_kernel(x_ref, g_ref, oa_ref, ob_ref, ocmp_ref, osel_ref, owin_ref, ngate_ref,
                  wbg_ref, wa_ref, wb_ref, wc_ref, wout_ref, gexp_ref, o_ref):
    x = x_ref[...]
    u = _rms(x, g_ref[...]).astype(BF16)
    ng = jax.nn.sigmoid(ngate_ref[...])
    gx = jnp.dot(ng, gexp_ref[...], preferred_element_type=F32, precision=HI)
    oc = gx[:, 0:256] * ocmp_ref[...] + gx[:, 256:512] * osel_ref[...] + gx[:, 512:768] * owin_ref[...]
    branches = ((oa_ref[...], wa_ref), (ob_ref[...], wb_ref), (oc, wc_ref))
    hsum = jnp.zeros(x.shape, F32)
    for k, (ob, w_ref) in enumerate(branches):
        gate = jax.nn.sigmoid(jnp.dot(u, wbg_ref[:, D_MODEL * k:D_MODEL * (k + 1)], preferred_element_type=F32))
        hsum = hsum + gate * jnp.dot(ob.astype(BF16), w_ref[...], preferred_element_type=F32)
    o_ref[...] = x + jnp.dot(hsum.astype(BF16), wout_ref[...], preferred_element_type=F32)


def _merge(x, g, o_a, o_b, o_cmp, o_sel, o_win, ngate, wbg, wa, wb, wc, wout, gexp, tm):
    n = x.shape[0]
    row = lambda w: pl.BlockSpec((tm, w), lambda i: (i, 0))
    full = lambda a: pl.BlockSpec(a.shape, lambda i: (0,) * a.ndim)
    return pl.pallas_call(
        _merge_kernel,
        grid=(n // tm,),
        in_specs=[row(D_MODEL), full(g), row(256), row(512), row(256), row(256), row(256), row(128),
                  full(wbg), full(wa), full(wb), full(wc), full(wout), full(gexp)],
        out_specs=row(D_MODEL),
        out_shape=jax.ShapeDtypeStruct((n, D_MODEL), F32),
        compiler_params=_cparams(("parallel",)),
        name="merge",
    )(x, g, o_a, o_b, o_cmp, o_sel, o_win, ngate, wbg, wa, wb, wc, wout, gexp)


def _rel_bucket(dist):
    n = jnp.maximum(dist, 0)
    exact = N_BUCKETS // 2
    nf = jnp.maximum(n, 1).astype(F32)
    large = exact + (jnp.log(nf / exact) / math.log(MAX_DISTANCE / exact) * (N_BUCKETS - exact)).astype(jnp.int32)
    return jnp.where(n < exact, n, jnp.minimum(large, N_BUCKETS - 1))


PAGES_PER_STEP = 16
N_PAGES = PAST_LEN // PAGE_SIZE
OWN_BLOCK = PAST_LEN // MOBA_BLOCK
CUR_SEL = PAST_LEN // NSA_SEL_BLOCK
IMP_W = 384


def _page_of(pt, b, slot, n_pool):
    b = jnp.clip(b, 0, pt.shape[0] - 1)
    return jnp.clip(pt[b, jnp.clip(slot, 0, N_PAGES - 1)], 0, n_pool - 1)


def _picked(sl, b, k):
    return sl[jnp.clip(b, 0, sl.shape[0] - 1), k]


def _row_of_heads(x, n):
    return jnp.concatenate([x[h:h + 1] for h in range(n)], axis=1)


def _diff_dec_kernel(pt_ref, q_ref, knew_ref, vnew_ref, bias_ref, b0_ref, lam_ref, subln_ref, *rest, lam_init):
    pages = rest[:PAGES_PER_STEP]
    o_ref, m_ref, l_ref, acc_ref = rest[PAGES_PER_STEP:]
    s = pl.program_id(1)

    @pl.when(s == 0)
    def _():
        for h in range(DIFF_HEADS):
            m_ref[8 * h:8 * (h + 1)] = (jnp.sum(q_ref[0, h] * knew_ref[0, h], axis=-1, keepdims=True)
                                        + b0_ref[h][:, 0:1])
            acc_ref[8 * h:8 * (h + 1)] = jnp.broadcast_to(vnew_ref[0, h], (8, DIFF_VDIM))
        l_ref[...] = jnp.ones(l_ref.shape, F32)

    logits = []
    for h in range(DIFF_HEADS):
        qh = q_ref[0, h].astype(BF16)
        logits.append(jnp.concatenate(
            [_dot_nt(qh, pages[r][pl.ds(h, PAGE_SIZE, stride=8), :].astype(BF16)) + bias_ref[h, r:r + 1, :]
             for r in range(PAGES_PER_STEP)], axis=1))
    alpha, p = _softmax_update(jnp.concatenate(logits, axis=0), m_ref, l_ref, narrow=False)
    pvs = []
    for h in range(DIFF_HEADS):
        pv = jnp.zeros((8, DIFF_VDIM), F32)
        for r in range(PAGES_PER_STEP):
            vh = pages[r][pl.ds(DIFF_HEADS + h, PAGE_SIZE, stride=8), :].astype(BF16)
            pv = pv + jnp.dot(p[8 * h:8 * (h + 1), PAGE_SIZE * r:PAGE_SIZE * (r + 1)].astype(BF16), vh,
                              preferred_element_type=F32)
        pvs.append(pv)
    acc_ref[...] = alpha * acc_ref[...] + jnp.concatenate(pvs, axis=0)

    @pl.when(s == pl.num_programs(1) - 1)
    def _():
        lam = lam_ref[...]
        lam_full = (jnp.exp(jnp.sum(lam[0:1] * lam[1:2], axis=-1, keepdims=True))
                    - jnp.exp(jnp.sum(lam[2:3] * lam[3:4], axis=-1, keepdims=True)) + lam_init)
        om = acc_ref[...] / l_ref[...]
        for h in range(DIFF_HEADS):
            d = om[8 * h:8 * h + 1] - lam_full * om[8 * h + 1:8 * h + 2]
            o_ref[0, :, DIFF_VDIM * h:DIFF_VDIM * (h + 1)] = _rms(d, subln_ref[...]) * (1.0 - lam_init)


def _diff_dec(l, dq, d_new, pages, page_table, bias_past, b0, lam, subln, lam_init):
    db = dq.shape[0]
    q4 = (dq * SCALE).reshape(db, DIFF_HEADS, 2, HEAD_DIM)
    q = jnp.zeros((db, DIFF_HEADS, 8, 128), F32)
    q = q.at[:, :, 0, :HEAD_DIM].set(q4[:, :, 0]).at[:, :, 1, HEAD_DIM:].set(q4[:, :, 1])
    knew = d_new[:, :512].reshape(db, DIFF_HEADS, 1, 128)
    vnew = d_new[:, 512:].reshape(db, DIFF_HEADS, 1, 128)
    page_specs = [
        pl.BlockSpec((None, None, PAGE_SIZE * 8, 128),
                     lambda b, s, pt, r=r: (l, _page_of(pt, b, PAGES_PER_STEP * s + r, pages.shape[1]), 0, 0))
        for r in range(PAGES_PER_STEP)]
    grid_spec = pltpu.PrefetchScalarGridSpec(
        num_scalar_prefetch=1,
        grid=(db, N_PAGES // PAGES_PER_STEP),
        in_specs=[
            pl.BlockSpec((1, DIFF_HEADS, 8, 128), lambda b, s, pt: (b, 0, 0, 0)),
            pl.BlockSpec((1, DIFF_HEADS, 1, 128), lambda b, s, pt: (b, 0, 0, 0)),
            pl.BlockSpec((1, DIFF_HEADS, 1, 128), lambda b, s, pt: (b, 0, 0, 0)),
            pl.BlockSpec((4, PAGES_PER_STEP, PAGE_SIZE), lambda b, s, pt: (1, s, 0)),
            pl.BlockSpec((4, 1, 128), lambda b, s, pt: (1, 0, 0)),
            pl.BlockSpec((4, HEAD_DIM), lambda b, s, pt: (0, 0)),
            pl.BlockSpec((1, DIFF_VDIM), lambda b, s, pt: (0, 0)),
        ] + page_specs,
        out_specs=pl.BlockSpec((1, 1, 512), lambda b, s, pt: (b, 0, 0)),
        scratch_shapes=[pltpu.VMEM((8 * DIFF_HEADS, 1), F32), pltpu.VMEM((8 * DIFF_HEADS, 1), F32),
                        pltpu.VMEM((8 * DIFF_HEADS, DIFF_VDIM), F32)],
    )
    out = pl.pallas_call(
        functools.partial(_diff_dec_kernel, lam_init=lam_init),
        grid_spec=grid_spec,
        out_shape=jax.ShapeDtypeStruct((db, 1, 512), F32),
        compiler_params=_cparams(("parallel", "arbitrary")),
        name="diff_decode",
    )(page_table, q, knew, vnew, bias_past, b0, lam, subln, *([pages] * PAGES_PER_STEP))
    return out.reshape(db, 512)


def _moba_gate_kernel(pt_ref, qcol_ref, *rest):
    pages = rest[:PAGES_PER_STEP]
    sel_ref, gate_ref = rest[PAGES_PER_STEP:]
    s = pl.program_id(1)
    lane = lax.broadcasted_iota(jnp.int32, (1, 1, 128), 2)

    @pl.when(s == 0)
    def _():
        gate_ref[...] = jnp.zeros(gate_ref.shape, F32)

    qb = qcol_ref[0]
    per_block = MOBA_BLOCK // PAGE_SIZE
    for blk in range(PAGES_PER_STEP // per_block):
        part = jnp.zeros((MOBA_HEADS, 8, PAGE_SIZE), F32)
        for half in range(per_block):
            prod = pages[per_block * blk + half][...] * qb
            part = part + jnp.sum(prod.reshape(MOBA_HEADS, HEAD_DIM // 8, 8, PAGE_SIZE), axis=1)
        tot = jnp.sum(jnp.sum(part, axis=2, keepdims=True), axis=1, keepdims=True) * (1.0 / MOBA_BLOCK)
        j = s * (PAGES_PER_STEP // per_block) + blk
        gate_ref[...] = jnp.where(lane == j, tot, gate_ref[...])

    @pl.when(s == pl.num_programs(1) - 1)
    def _():
        score = jnp.where(lane < OWN_BLOCK, gate_ref[...], NEG)
        rank = _rank_desc(score, OWN_BLOCK)
        lane_f = lane.astype(F32)
        out = jnp.zeros(score.shape, jnp.int32)
        for r in range(MOBA_TOPK):
            idx = jnp.sum(jnp.where(rank == r, lane_f, 0.0), axis=2, keepdims=True).astype(jnp.int32)
            out = jnp.where(lane == r, idx, out)
        sel_ref[0] = out


def _moba_gate(l, mq, pages, page_table):
    db = mq.shape[0]
    qcol = jnp.broadcast_to(mq.reshape(db, MOBA_HEADS, HEAD_DIM, 1), (db, MOBA_HEADS, HEAD_DIM, PAGE_SIZE))
    page_specs = [
        pl.BlockSpec((None, None, None, MOBA_HEADS, HEAD_DIM, PAGE_SIZE),
                     lambda b, s, pt, r=r: (l, _page_of(pt, b, PAGES_PER_STEP * s + r, pages.shape[1]), 0, 0, 0, 0))
        for r in range(PAGES_PER_STEP)]
    grid_spec = pltpu.PrefetchScalarGridSpec(
        num_scalar_prefetch=1,
        grid=(db, N_PAGES // PAGES_PER_STEP),
        in_specs=[pl.BlockSpec((1, MOBA_HEADS, HEAD_DIM, PAGE_SIZE), lambda b, s, pt: (b, 0, 0, 0))] + page_specs,
        out_specs=pl.BlockSpec((1, MOBA_HEADS, 1, 128), lambda b, s, pt: (b, 0, 0, 0)),
        scratch_shapes=[pltpu.VMEM((MOBA_HEADS, 1, 128), F32)],
    )
    sel = pl.pallas_call(
        _moba_gate_kernel,
        grid_spec=grid_spec,
        out_shape=jax.ShapeDtypeStruct((db, MOBA_HEADS, 1, 128), jnp.int32),
        compiler_params=_cparams(("parallel", "arbitrary")),
        name="moba_gate",
    )(page_table, qcol, *([pages] * PAGES_PER_STEP))
    sel = jnp.clip(sel[:, :, 0, :MOBA_TOPK].reshape(db, MOBA_HEADS * MOBA_TOPK), 0, OWN_BLOCK - 1)
    return jnp.pad(sel, ((0, 0), (0, 128 - MOBA_HEADS * MOBA_TOPK)))


def _moba_dec_kernel(pt_ref, sel_ref, q_ref, knew_ref, vnew_ref, bias_ref, b0_ref, *rest):
    per_head = MOBA_TOPK * (MOBA_BLOCK // PAGE_SIZE)
    pages = rest[:MOBA_HEADS * per_head]
    o_ref = rest[MOBA_HEADS * per_head]
    b = pl.program_id(0)
    outs = []
    for h in range(MOBA_HEADS):
        qh = q_ref[0, h]
        qb = qh.astype(BF16)
        s_self = jnp.sum(qh * knew_ref[0, h], axis=-1, keepdims=True) + b0_ref[h][:, 0:1]
        logits = []
        for i in range(per_head):
            page = jnp.clip((MOBA_BLOCK // PAGE_SIZE) * sel_ref[b, MOBA_TOPK * h + i // 2] + i % 2, 0, N_PAGES - 1)
            kt = pages[per_head * h + i][0].astype(BF16)
            logits.append(jnp.dot(qb, kt, preferred_element_type=F32) + bias_ref[h, pl.ds(page, 1), :])
        sc = jnp.concatenate(logits, axis=1)
        m = jnp.maximum(jnp.max(sc, axis=-1, keepdims=True), s_self)
        p = jnp.exp(sc - m)
        p_self = jnp.exp(s_self - m)
        den = jnp.sum(p, axis=-1, keepdims=True) + p_self
        pv = p_self * vnew_ref[0, h]
        for i in range(per_head):
            vt = pages[per_head * h + i][1].astype(BF16)
            pv = pv + _dot_nt(p[:, PAGE_SIZE * i:PAGE_SIZE * (i + 1)].astype(BF16), vt)
        outs.append((pv / den)[0:1])
    o_ref[0] = jnp.concatenate(outs, axis=1)


def _moba_dec(l, mq, m_new, sel, pages, page_table, bias_past, b0):
    db = mq.shape[0]
    q = jnp.zeros((db, MOBA_HEADS, 8, HEAD_DIM), F32).at[:, :, 0].set((mq * SCALE).reshape(db, MOBA_HEADS, HEAD_DIM))
    knew = m_new[:, :256].reshape(db, MOBA_HEADS, 1, HEAD_DIM)
    vnew = m_new[:, 256:].reshape(db, MOBA_HEADS, 1, HEAD_DIM)
    per_block = MOBA_BLOCK // PAGE_SIZE
    n_pool = pages.shape[1]
    page_specs = [
        pl.BlockSpec((None, None, 2, None, HEAD_DIM, PAGE_SIZE),
                     lambda b, pt, sl, h=h, r=r, half=half:
                     (l, _page_of(pt, b, per_block * _picked(sl, b, MOBA_TOPK * h + r) + half, n_pool), 0, h, 0, 0))
        for h in range(MOBA_HEADS) for r in range(MOBA_TOPK) for half in range(per_block)]
    grid_spec = pltpu.PrefetchScalarGridSpec(
        num_scalar_prefetch=2,
        grid=(db,),
        in_specs=[
            pl.BlockSpec((1, MOBA_HEADS, 8, HEAD_DIM), lambda b, pt, sl: (b, 0, 0, 0)),
            pl.BlockSpec((1, MOBA_HEADS, 1, HEAD_DIM), lambda b, pt, sl: (b, 0, 0, 0)),
            pl.BlockSpec((1, MOBA_HEADS, 1, HEAD_DIM), lambda b, pt, sl: (b, 0, 0, 0)),
            pl.BlockSpec((4, N_PAGES, PAGE_SIZE), lambda b, pt, sl: (0, 0, 0)),
            pl.BlockSpec((4, 1, 128), lambda b, pt, sl: (0, 0, 0)),
        ] + page_specs,
        out_specs=pl.BlockSpec((1, 1, 256), lambda b, pt, sl: (b, 0, 0)),
    )
    out = pl.pallas_call(
        _moba_dec_kernel,
        grid_spec=grid_spec,
        out_shape=jax.ShapeDtypeStruct((db, 1, 256), F32),
        compiler_params=_cparams(("parallel",)),
        name="moba_decode",
    )(page_table, sel, q, knew, vnew, bias_past, b0, *([pages] * len(page_specs)))
    return out.reshape(db, 256)


def _nsa_cmp_dec_kernel(pt_ref, q_ref, w1_ref, pos_ref, w1raw_ref, w2_ref, gkc_ref, ovl_ref, *rest):
    pages = rest[:PAGES_PER_STEP]
    ocmp_ref, sel_ref, xt_ref, cst_ref, carry_ref, m_ref, l_ref, acc_ref, imp_ref = rest[PAGES_PER_STEP:]
    b = pl.program_id(0)
    s = pl.program_id(1)
    rows = PAGES_PER_STEP * PAGE_SIZE // NSA_CMP_STRIDE

    @pl.when((b == 0) & (s == 0))
    def _():
        for kv in range(2):
            pos = jnp.broadcast_to(pos_ref[kv], (8, pos_ref.shape[-1]))
            cst_ref[:, 128 * kv:128 * (kv + 1)] = jnp.dot(
                pos, w1raw_ref[kv], preferred_element_type=F32, precision=HI)[0:1]

    @pl.when(s == 0)
    def _():
        _init_state(m_ref, l_ref, acc_ref)
        imp_ref[...] = jnp.zeros(imp_ref.shape, F32)
        carry_ref[...] = jnp.zeros(carry_ref.shape, F32)

    for r in range(PAGES_PER_STEP):
        xt_ref[PAGE_SIZE * r:PAGE_SIZE * (r + 1), :] = pages[r][...].reshape(2 * HEAD_DIM, PAGE_SIZE).T
    a = jnp.zeros((rows, 512), F32)
    for rr in range(NSA_CMP_STRIDE):
        lhs = xt_ref[pl.ds(rr, rows, stride=NSA_CMP_STRIDE), :].astype(BF16)
        a = a + jnp.dot(lhs, w1_ref[rr], preferred_element_type=F32)
    first = lax.broadcasted_iota(jnp.int32, (rows, 1), 0) == 0
    outs = []
    for kv in range(2):
        top = a[:, 256 * kv:256 * kv + 128]
        bot = a[:, 256 * kv + 128:256 * kv + 256]
        top_prev = jnp.where(first, carry_ref[:, 128 * kv:128 * (kv + 1)], pltpu.roll(top, 1, 0))
        hid = top_prev + bot + cst_ref[:, 128 * kv:128 * (kv + 1)]
        outs.append(jnp.dot(jax.nn.gelu(hid), w2_ref[kv], preferred_element_type=F32, precision=HI))
        carry_ref[:, 128 * kv:128 * (kv + 1)] = top[rows - 1:rows]
    kc = _rms(outs[0], gkc_ref[...])
    vc = outs[1].astype(BF16)

    q = q_ref[0] * SCALE
    col = lax.broadcasted_iota(jnp.int32, (1, rows), 1)
    sc = jnp.where(col + s > 0, _dot_nt(q, kc, precision=HI), NEG)
    m_old = m_ref[0]
    m_new = jnp.maximum(m_old, jnp.max(sc, axis=-1, keepdims=True))
    alpha = jnp.exp(m_old - m_new)
    p = jnp.exp(sc - m_new)
    l_ref[0] = alpha * l_ref[0] + jnp.sum(p, axis=-1, keepdims=True)
    acc_ref[0] = alpha * acc_ref[0] + jnp.dot(p.astype(BF16), vc, preferred_element_type=F32)
    imp_ref[...] = alpha * imp_ref[...] + jnp.dot(p, ovl_ref[...], preferred_element_type=F32, precision=HI)
    m_ref[0] = m_new

    @pl.when(s == pl.num_programs(1) - 1)
    def _():
        ocmp_ref[0] = _row_of_heads(acc_ref[0] / l_ref[0], NSA_HEADS)
        impn = imp_ref[...] / l_ref[0]
        imp = impn[0:1] + impn[1:2] + impn[2:3] + impn[3:4]
        lane = lax.broadcasted_iota(jnp.int32, (1, IMP_W), 1)
        forced = (lane == 0) | (lane == CUR_SEL) | (lane == CUR_SEL - 1)
        score = jnp.where(lane <= CUR_SEL, imp + jnp.where(forced, FORCE_BONUS, 0.0), NEG)
        rank = _rank_desc(score, CUR_SEL + 1)
        lane_f = lane.astype(F32)
        lane_o = lax.broadcasted_iota(jnp.int32, (1, 128), 1)
        out = jnp.zeros((1, 128), jnp.int32)
        for r in range(NSA_SEL_TOPK):
            idx = jnp.sum(jnp.where(rank == r, lane_f, 0.0), axis=1, keepdims=True).astype(jnp.int32)
            out = jnp.where(lane_o == r, idx, out)
        sel_ref[0] = out


def _nsa_cmp_dec(l, nq, pages, page_table, w1dec, pos_flat, w1raw, w2, gkc, ovl_dec):
    db = nq.shape[0]
    q = jnp.zeros((db, 8, HEAD_DIM), F32).at[:, :NSA_HEADS].set(nq.reshape(db, NSA_HEADS, HEAD_DIM))
    rows = PAGES_PER_STEP * PAGE_SIZE // NSA_CMP_STRIDE
    page_specs = [
        pl.BlockSpec((None, None, 2, HEAD_DIM, PAGE_SIZE),
                     lambda b, s, pt, r=r: (l, _page_of(pt, b, PAGES_PER_STEP * s + r, pages.shape[1]), 0, 0, 0))
        for r in range(PAGES_PER_STEP)]
    full = lambda a: pl.BlockSpec(a.shape, lambda b, s, pt: (0,) * a.ndim)
    grid_spec = pltpu.PrefetchScalarGridSpec(
        num_scalar_prefetch=1,
        grid=(db, N_PAGES // PAGES_PER_STEP),
        in_specs=[pl.BlockSpec((1, 8, HEAD_DIM), lambda b, s, pt: (b, 0, 0)),
                  full(w1dec), full(pos_flat), full(w1raw), full(w2), full(gkc),
                  pl.BlockSpec((rows, IMP_W), lambda b, s, pt: (s, 0))] + page_specs,
        out_specs=[pl.BlockSpec((1, 1, 256), lambda b, s, pt: (b, 0, 0)),
                   pl.BlockSpec((1, 1, 128), lambda b, s, pt: (b, 0, 0))],
        scratch_shapes=[
            pltpu.VMEM((PAGES_PER_STEP * PAGE_SIZE, 128), F32),
            pltpu.VMEM((1, 256), F32),
            pltpu.VMEM((1, 256), F32),
            pltpu.VMEM((1, 8, 1), F32),
            pltpu.VMEM((1, 8, 1), F32),
            pltpu.VMEM((1, 8, HEAD_DIM), F32),
            pltpu.VMEM((8, IMP_W), F32),
        ],
    )
    o_cmp, sel = pl.pallas_call(
        _nsa_cmp_dec_kernel,
        grid_spec=grid_spec,
        out_shape=[jax.ShapeDtypeStruct((db, 1, 256), F32), jax.ShapeDtypeStruct((db, 1, 128), jnp.int32)],
        compiler_params=_cparams(("arbitrary", "arbitrary")),
        name="nsa_cmp_decode",
    )(page_table, q, w1dec, pos_flat, w1raw, w2, gkc, ovl_dec, *([pages] * PAGES_PER_STEP))
    return o_cmp.reshape(db, 256), jnp.clip(sel[:, 0, :], 0, CUR_SEL)


def _nsa_dec_kernel(pt_ref, sel_ref, q_ref, snew_ref, wnew_ref, wcol_ref, bias_ref, bwin_ref, b0_ref, win_ref,
                    *rest):
    pages = rest[:NSA_SEL_TOPK]
    osel_ref, owin_ref, buf_ref = rest[NSA_SEL_TOPK:]
    b = pl.program_id(0)
    q = q_ref[0] * SCALE
    qb = q.astype(BF16)
    b0 = b0_ref[:, 0:1]

    half_of = lax.broadcasted_iota(jnp.int32, (1, PAGE_SIZE), 1) // NSA_SEL_BLOCK
    logits = []
    for r in range(NSA_SEL_TOPK):
        j = sel_ref[b, r]
        page = jnp.clip(j // 2, 0, N_PAGES - 1)
        want = jnp.where(j < CUR_SEL, j % 2, 2)
        sc = jnp.dot(qb, pages[r][0].astype(BF16), preferred_element_type=F32) + bias_ref[page]
        logits.append(jnp.where(half_of == want, sc, NEG))
    sc = jnp.concatenate(logits, axis=1)
    s_self = jnp.sum(q * snew_ref[0, 0:1], axis=-1, keepdims=True) + b0
    m = jnp.maximum(jnp.max(sc, axis=-1, keepdims=True), s_self)
    p = jnp.exp(sc - m)
    p_self = jnp.exp(s_self - m)
    den = jnp.sum(p, axis=-1, keepdims=True) + p_self
    pv = p_self * snew_ref[0, 1:2]
    for r in range(NSA_SEL_TOPK):
        pv = pv + _dot_nt(p[:, PAGE_SIZE * r:PAGE_SIZE * (r + 1)].astype(BF16), pages[r][1].astype(BF16))
    osel_ref[0] = _row_of_heads(pv / den, NSA_HEADS)

    wlen = win_ref.shape[-1]
    lane = lax.broadcasted_iota(jnp.int32, (1, wlen), 1)
    sw = jnp.dot(qb, win_ref[0].astype(BF16), preferred_element_type=F32) + bwin_ref[...]
    sw = jnp.where(lane == 0, NEG, sw)
    s_self = jnp.sum(q * wnew_ref[0, 0:1], axis=-1, keepdims=True) + b0
    m = jnp.maximum(jnp.max(sw, axis=-1, keepdims=True), s_self)
    p = jnp.exp(sw - m)
    p_self = jnp.exp(s_self - m)
    den = jnp.sum(p, axis=-1, keepdims=True) + p_self
    pv = p_self * wnew_ref[0, 1:2] + _dot_nt(p.astype(BF16), win_ref[1].astype(BF16))
    owin_ref[0] = _row_of_heads(pv / den, NSA_HEADS)
    last = lax.broadcasted_iota(jnp.int32, (1, 128), 1) == 127
    for kv in range(2):
        shifted = pltpu.roll(win_ref[kv], wlen - 1, 1)
        buf_ref[0, kv, :, 0:wlen - 128] = shifted[:, 0:wlen - 128]
        buf_ref[0, kv, :, wlen - 128:wlen] = jnp.where(last, wcol_ref[0, kv], shifted[:, wlen - 128:wlen])


def _nsa_dec(l, nq, n_new, w_new, sel, pages, win, page_table, bias_nsa, bwin, b0n):
    db = nq.shape[0]
    wlen = win.shape[-1]
    q = jnp.zeros((db, 8, HEAD_DIM), F32).at[:, :NSA_HEADS].set(nq.reshape(db, NSA_HEADS, HEAD_DIM))
    snew = n_new[:, 128:].reshape(db, 2, HEAD_DIM)
    wnew = w_new.reshape(db, 2, HEAD_DIM)
    wcol = jnp.broadcast_to(w_new.reshape(db, 2, HEAD_DIM, 1), (db, 2, HEAD_DIM, 128))
    n_pool = pages.shape[1]
    page_specs = [
        pl.BlockSpec((None, None, 2, HEAD_DIM, PAGE_SIZE),
                     lambda b, pt, sl, r=r: (l, _page_of(pt, b, _picked(sl, b, r) // 2, n_pool), 1, 0, 0))
        for r in range(NSA_SEL_TOPK)]
    full = lambda a: pl.BlockSpec(a.shape, lambda b, pt, sl: (0,) * a.ndim)
    grid_spec = pltpu.PrefetchScalarGridSpec(
        num_scalar_prefetch=2,
        grid=(db,),
        in_specs=[pl.BlockSpec((1, 8, HEAD_DIM), lambda b, pt, sl: (b, 0, 0)),
                  pl.BlockSpec((1, 2, HEAD_DIM), lambda b, pt, sl: (b, 0, 0)),
                  pl.BlockSpec((1, 2, HEAD_DIM), lambda b, pt, sl: (b, 0, 0)),
                  pl.BlockSpec((1, 2, HEAD_DIM, 128), lambda b, pt, sl: (b, 0, 0, 0)),
                  full(bias_nsa), full(bwin), full(b0n),
                  pl.BlockSpec((None, None, 2, HEAD_DIM, wlen), lambda b, pt, sl: (l, b, 0, 0, 0))] + page_specs,
        out_specs=[pl.BlockSpec((1, 1, 256), lambda b, pt, sl: (b, 0, 0)),
                   pl.BlockSpec((1, 1, 256), lambda b, pt, sl: (b, 0, 0)),
                   pl.BlockSpec((1, 2, HEAD_DIM, wlen), lambda b, pt, sl: (b, 0, 0, 0))],
    )
    o_sel, o_win, buf = pl.pallas_call(
        _nsa_dec_kernel,
        grid_spec=grid_spec,
        out_shape=[jax.ShapeDtypeStruct((db, 1, 256), F32), jax.ShapeDtypeStruct((db, 1, 256), F32),
                   jax.ShapeDtypeStruct((db, 2, HEAD_DIM, wlen), F32)],
        compiler_params=_cparams(("parallel",)),
        name="nsa_decode",
    )(page_table, sel, q, snew, wnew, wcol, bias_nsa, bwin, b0n, win, *([pages] * NSA_SEL_TOPK))
    return o_sel.reshape(db, 256), o_win.reshape(db, 256), jnp.transpose(buf, (0, 3, 1, 2))


def _decode_bias(rel_bias, wlen):
    flat = rel_bias.T[:, _rel_bucket(PAST_LEN - jnp.arange(PAST_LEN))]
    past = flat.reshape(-1, N_PAGES, PAGE_SIZE)
    first_nsa = MOBA_HEADS + DIFF_HEADS
    bias_nsa = jnp.pad(jnp.transpose(past[first_nsa:], (1, 0, 2)), ((0, 0), (0, 8 - NSA_HEADS), (0, 0)))
    bwin = jnp.pad(flat[first_nsa:, PAST_LEN - wlen:], ((0, 8 - NSA_HEADS), (0, 0)))
    zero = rel_bias[_rel_bucket(jnp.zeros((), jnp.int32))]
    b0 = jnp.broadcast_to(zero[:, None, None], (zero.shape[0], 1, 128))
    b0n = jnp.pad(jnp.broadcast_to(zero[first_nsa:, None], (NSA_HEADS, 128)), ((0, 8 - NSA_HEADS), (0, 0)))
    return past, bias_nsa, bwin, b0, b0n


def _decode_overlap():
    n = np.arange(N_PAGES * PAGE_SIZE // NSA_CMP_STRIDE) - 1
    lo = np.arange(IMP_W) * NSA_SEL_BLOCK
    start = n[:, None] * NSA_CMP_STRIDE
    ovl = (start < lo[None, :] + NSA_SEL_BLOCK) & (start + NSA_CMP_BLOCK > lo[None, :]) & (n[:, None] >= 0)
    return jnp.asarray(ovl, F32)


def _pack_layer(l, norm_gain, ffn_w1, ffn_w2, w_in, qk_gain, cmp_pos, cmp_w1, cmp_w2,
                w_branch_moba, w_branch_diff, w_branch_nsa, w_out):
    w = w_in[l]
    w_pack = jnp.concatenate(
        [w[:, :N_MAIN], jnp.pad(w[:, N_MAIN:N_MAIN + 12], ((0, 0), (0, 116)))], axis=1).astype(BF16)
    g = qk_gain[l]
    ones = jnp.ones((HEAD_DIM,), F32)
    zeros = jnp.zeros((HEAD_DIM,), F32)

    def rep(v, k):
        return jnp.tile(v, k)

    gain_row = jnp.concatenate([
        rep(g[0], 4), rep(g[1], 4), rep(ones, 4), rep(g[2], 8), rep(g[3], 8), rep(ones, 8), rep(g[4], 4),
        ones, ones, g[6], ones, g[7], ones])[None, :]
    nmask_row = jnp.concatenate([
        rep(ones, 8), rep(zeros, 4), rep(ones, 16), rep(zeros, 8), rep(ones, 4),
        zeros, zeros, ones, zeros, ones, zeros])[None, :]
    w1 = cmp_w1[l].reshape(2, 2, NSA_CMP_STRIDE, HEAD_DIM, NSA_CMP_HIDDEN)
    zero = jnp.zeros((NSA_CMP_STRIDE, HEAD_DIM, NSA_CMP_HIDDEN), F32)
    k_cols = jnp.concatenate([w1[0, 0], w1[0, 1], zero, zero], axis=-1)
    v_cols = jnp.concatenate([zero, zero, w1[1, 0], w1[1, 1]], axis=-1)
    w1cat = jnp.concatenate([k_cols, v_cols], axis=1).reshape(NSA_CMP_STRIDE * 128, 512)
    return dict(
        g0=norm_gain[l, 0][None, :], g1=norm_gain[l, 1][None, :], g2=norm_gain[l, 2][None, :],
        ffn1=(ffn_w1[l, 0].astype(BF16), ffn_w2[l, 0].astype(BF16)),
        ffn2=(ffn_w1[l, 1].astype(BF16), ffn_w2[l, 1].astype(BF16)),
        w_pack=w_pack, gain_row=gain_row, nmask_row=nmask_row,
        wbg=w[:, N_MAIN + 12:].astype(BF16),
        w1cat=w1cat, pos_flat=cmp_pos[l].reshape(2, 1, NSA_CMP_BLOCK * HEAD_DIM), w1raw=cmp_w1[l],
        w2x4=jnp.tile(cmp_w2[l], (1, 1, 4)), gkc4=jnp.tile(g[5], 4)[None, :],
        wa=w_branch_moba[l].astype(BF16), wb=w_branch_diff[l].astype(BF16), wc=w_branch_nsa[l].astype(BF16),
        wout=w_out[l].astype(BF16),
    )


def _bias_tiles(rel_bias):
    qi = jnp.arange(TQ)[:, None]
    ki = jnp.arange(TQ)[None, :]
    dist = jnp.stack([jnp.maximum(qi - ki, 0), TQ + qi - ki, 2 * TQ + qi - ki])
    onehot = (_rel_bucket(dist)[..., None] == jnp.arange(N_BUCKETS)).astype(F32)
    tiles = jnp.einsum('tqkn,nh->htqk', onehot, rel_bias.astype(F32), precision=HI)
    causal = jnp.stack([qi >= ki, jnp.ones((TQ, TQ), bool), jnp.ones((TQ, TQ), bool)])
    return jnp.where(causal, tiles, NEG)


def _const_tables(seq):
    blocks = np.arange(seq // NSA_CMP_STRIDE) * NSA_CMP_STRIDE
    lo = np.arange(seq // NSA_SEL_BLOCK) * NSA_SEL_BLOCK
    overlap = ((blocks[:, None] < lo[None, :] + NSA_SEL_BLOCK) & (blocks[:, None] + NSA_CMP_BLOCK > lo[None, :]))
    expand = np.arange(seq // NSA_SEL_BLOCK)[:, None] == (np.arange(seq)[None, :] // NSA_SEL_BLOCK)
    gexp = np.zeros((128, 768), np.float32)
    for k in range(3):
        for h in range(NSA_HEADS):
            gexp[k * NSA_HEADS + h, 256 * k + HEAD_DIM * h:256 * k + HEAD_DIM * (h + 1)] = 1.0
    bd = np.kron(np.eye(4, dtype=np.float32), np.ones((HEAD_DIM, HEAD_DIM), np.float32))
    return (jnp.asarray(overlap, F32), jnp.asarray(expand, BF16), jnp.asarray(gexp), jnp.asarray(bd, BF16))


def kernel(x_prompt, x_sample, cache_moba_kv, cache_diff_kv, cache_nsa_kv, state_nsa_win, page_table, rel_bias, norm_gain, ffn_w1, ffn_w2, w_in, qk_gain, diff_lambda, diff_subln, cmp_pos, cmp_w1, cmp_w2, w_branch_moba, w_branch_diff, w_branch_nsa, w_out):
    bsz, seq, _ = x_prompt.shape
    db, dq_len, _ = x_sample.shape
    n_p = bsz * seq
    n_s = db * dq_len
    assert dq_len == 1 and OWN_BLOCK >= MOBA_TOPK and CUR_SEL + 1 >= NSA_SEL_TOPK
    bias_tiles = _bias_tiles(rel_bias)
    overlap, expand, gexp, bd = _const_tables(seq)
    n_pool = cache_diff_kv.shape[1]
    moba_pages = jnp.transpose(cache_moba_kv, (0, 1, 3, 4, 5, 2))
    diff_pages = cache_diff_kv.reshape(DEPTH, n_pool, PAGE_SIZE * 8, 128)
    nsa_pages = jnp.transpose(cache_nsa_kv, (0, 1, 3, 4, 2))
    win_state = jnp.transpose(state_nsa_win, (0, 1, 3, 4, 2))
    bias_past, bias_nsa, bwin, b0, b0n = _decode_bias(rel_bias, win_state.shape[-1])
    ovl_dec = _decode_overlap()

    xp = x_prompt.reshape(n_p, D_MODEL)
    xs = x_sample.reshape(n_s, D_MODEL)
    outs = [[] for _ in range(8)]
    for l in range(DEPTH):
        lam_init = 0.8 - 0.6 * math.exp(-0.3 * l)
        pk = _pack_layer(l, norm_gain, ffn_w1, ffn_w2, w_in, qk_gain, cmp_pos, cmp_w1, cmp_w2,
                         w_branch_moba, w_branch_diff, w_branch_nsa, w_out)
        xp = _ffn(xp, pk['g0'], *pk['ffn1'], tm=1024)
        xs = _ffn(xs, pk['g0'], *pk['ffn1'], tm=n_s)

        mq, m_row, dq, d_row, nq, n_row, w_row, ngate = _inproj(
            xp, pk['g1'], pk['w_pack'], pk['gain_row'], pk['nmask_row'], bd, tm=512)
        r3 = lambda a: a.reshape(bsz, seq, a.shape[-1])
        mq, m_row, dq, d_row, nq, n_row, w_row = map(r3, (mq, m_row, dq, d_row, nq, n_row, w_row))
        o_a = _moba_prompt(mq, m_row, _moba_kmean(m_row), bias_tiles)
        o_b = _diff_prompt(dq, d_row, bias_tiles, diff_lambda[l], diff_subln[l][None, :], lam_init)
        kc4, vc4 = _nsa_compress(n_row[:, :, :128], pk['w1cat'], pk['pos_flat'], pk['w1raw'], pk['w2x4'], pk['gkc4'])
        o_cmp, selmask = _nsa_cmp_select(nq, kc4, vc4, overlap)
        o_sel = _nsa_selected(nq, n_row, selmask, expand, bias_tiles)
        o_win = _nsa_window(nq, w_row, bias_tiles)
        f2 = lambda a: a.reshape(n_p, a.shape[-1])
        xp = _merge(xp, pk['g1'], f2(o_a), f2(o_b), f2(o_cmp), f2(o_sel), f2(o_win), ngate,
                    pk['wbg'], pk['wa'], pk['wb'], pk['wc'], pk['wout'], gexp, tm=256)
        outs[0].append(m_row.reshape(bsz, seq, 2, MOBA_HEADS, HEAD_DIM))
        outs[2].append(d_row.reshape(bsz, seq, 2, DIFF_HEADS, DIFF_VDIM))
        outs[4].append(n_row.reshape(bsz, seq, 4, HEAD_DIM))
        outs[6].append(w_row[:, seq - min(NSA_WINDOW, seq):].reshape(bsz, min(NSA_WINDOW, seq), 2, HEAD_DIM))

        mq, m_row, dq, d_row, nq, n_row, w_row, ngate = _inproj(
            xs, pk['g1'], pk['w_pack'], pk['gain_row'], pk['nmask_row'], bd, tm=n_s)
        o_a = _moba_dec(l, mq, m_row, _moba_gate(l, mq, moba_pages, page_table), moba_pages, page_table,
                        bias_past, b0)
        o_b = _diff_dec(l, dq, d_row, diff_pages, page_table, bias_past, b0, diff_lambda[l],
                        diff_subln[l][None, :], lam_init)
        o_cmp, nsa_sel = _nsa_cmp_dec(l, nq, nsa_pages, page_table, pk['w1cat'].reshape(NSA_CMP_STRIDE, 128, 512)
                                      .astype(BF16), pk['pos_flat'], pk['w1raw'], cmp_w2[l], qk_gain[l, 5][None, :],
                                      ovl_dec)
        o_sel, o_win, buf = _nsa_dec(l, nq, n_row, w_row, nsa_sel, nsa_pages, win_state, page_table,
                                     bias_nsa, bwin, b0n)
        xs = _merge(xs, pk['g1'], o_a, o_b, o_cmp, o_sel, o_win, ngate,
                    pk['wbg'], pk['wa'], pk['wb'], pk['wc'], pk['wout'], gexp, tm=n_s)
        outs[1].append(m_row.reshape(db, dq_len, 2, MOBA_HEADS, HEAD_DIM))
        outs[3].append(d_row.reshape(db, dq_len, 2, DIFF_HEADS, DIFF_VDIM))
        outs[5].append(n_row.reshape(db, dq_len, 4, HEAD_DIM))
        outs[7].append(buf)

        xp = _ffn(xp, pk['g2'], *pk['ffn2'], tm=1024)
        xs = _ffn(xs, pk['g2'], *pk['ffn2'], tm=n_s)
    return (xp.reshape(bsz, seq, D_MODEL), xs.reshape(db, dq_len, D_MODEL)) + tuple(jnp.stack(o) for o in outs)
```
